```python
import math
import jax, jax.numpy as jnp
from jax import lax
import numpy as np

D_MODEL = 1024
BATCH = 8
SEQ = 4096
DEPTH = 2
DEC_BATCH = 16
DEC_SEQ = 32
PAST_LEN = 1024

CHUNK = 64
POOL_WINDOWS = (2, 4, 8, 16)
N_POOL_GROUPS = 4
POOL_WIDTH = D_MODEL
POOL_GROUP = POOL_WIDTH // N_POOL_GROUPS
POOL_STATE = 15
HEAD_DK = 128
HEAD_DV = 128
N_HEADS = D_MODEL // HEAD_DV
QK_WIDTH = N_HEADS * HEAD_DK
V_WIDTH = N_HEADS * HEAD_DV
CONV_WIDTH = 4
CONV_CH = 2 * QK_WIDTH + V_WIDTH
D_FF = 4 * D_MODEL
ALPHA = (2 * DEPTH) ** 0.25
BETA_INIT = (8 * DEPTH) ** -0.25
LN_EPS = 1e-5
RMS_EPS = 1e-6
L2_EPS = 1e-6
OFF_POOL = POOL_WIDTH
OFF_QKV = OFF_POOL + CONV_CH
OFF_Z = OFF_QKV + V_WIDTH
OFF_GA = OFF_Z + D_MODEL
OFF_GB = OFF_GA + D_MODEL
OFF_BETA = OFF_GB + N_HEADS
IN_WIDTH = OFF_BETA + N_HEADS

kernel_name = "hybrid_pool_gdn_streaming_step"


def layer_norm(x, g, b):
    xf = x.astype(jnp.float32)
    mu = jnp.mean(xf, -1, keepdims=True)
    var = jnp.mean(jnp.square(xf - mu), -1, keepdims=True)
    return ((xf - mu) * lax.rsqrt(var + LN_EPS) * g.astype(jnp.float32) + b.astype(jnp.float32)).astype(x.dtype)


def l2_normalize(x):
    xf = x.astype(jnp.float32)
    return xf * lax.rsqrt(jnp.sum(xf * xf, -1, keepdims=True) + L2_EPS)


def pool_mixer(u_ext, pos0, w_pool, pool_scale):
    B, Lx, _ = u_ext.shape
    L = Lx - POOL_STATE
    uf = u_ext.astype(jnp.float32)
    cs = jnp.concatenate([jnp.zeros((B, 1, POOL_WIDTH), jnp.float32), jnp.cumsum(uf, axis=1)], axis=1)
    end = cs[:, POOL_STATE + 1:]
    pos = pos0 + jnp.arange(L)
    means = []
    for gi, w in enumerate(POOL_WINDOWS):
        lo, hi = gi * POOL_GROUP, (gi + 1) * POOL_GROUP
        start = cs[:, POOL_STATE + 1 - w: POOL_STATE + 1 - w + L, lo:hi]
        cnt = jnp.minimum(w, pos + 1).astype(jnp.float32)[None, :, None]
        means.append((end[..., lo:hi] - start) / cnt)
    mixed = (jnp.concatenate(means, -1) - uf[:, POOL_STATE:]).astype(u_ext.dtype)
    mixed = mixed.reshape(B, L, N_POOL_GROUPS, POOL_GROUP)
    y = jnp.einsum('blgc,gcd->blgd', mixed, w_pool).reshape(B, L, POOL_WIDTH)
    return y * pool_scale


def causal_short_conv(x_ext, conv_w):
    L = x_ext.shape[1] - (CONV_WIDTH - 1)
    y = x_ext[:, 0:L] * conv_w[0]
    for t in range(1, CONV_WIDTH):
        y = y + x_ext[:, t:t + L] * conv_w[t]
    return jax.nn.silu(y)


def gated_delta_rule(q, k, v, g, beta, S0):
    B, L, H, DK = q.shape
    DV = v.shape[-1]
    C = min(CHUNK, L)
    n = L // C
    f32 = jnp.float32

    def chunks4(t):
        return jnp.moveaxis(t.astype(f32).reshape(B, n, C, H, t.shape[-1]), 3, 2)

    def chunks3(t):
        return jnp.moveaxis(t.astype(f32).reshape(B, n, C, H), 3, 2)

    qc, kc, vc = chunks4(q), chunks4(k), chunks4(v)
    gc = jnp.cumsum(chunks3(g), axis=-1)
    bc = chunks3(beta)
    idx = jnp.arange(C)
    incl = idx[:, None] >= idx[None, :]
    strict = idx[:, None] > idx[None, :]
    decay = jnp.exp(jnp.where(incl, gc[..., :, None] - gc[..., None, :], -jnp.inf))
    kb = kc * bc[..., None]
    lower = jnp.where(strict, jnp.einsum('bnhid,bnhjd->bnhij', kb, kc) * decay, 0.0)
    a_mat = jnp.eye(C, dtype=f32) + lower
    rhs = jnp.concatenate([vc * bc[..., None], kb * jnp.exp(gc)[..., None]], axis=-1)
    sol = lax.linalg.triangular_solve(a_mat, rhs, left_side=True, lower=True, unit_diagonal=True)
    u_val, w_dec = sol[..., :DV], sol[..., DV:]
    attn_in = jnp.einsum('bnhid,bnhjd->bnhij', qc, kc) * decay
    q_dec = qc * jnp.exp(gc)[..., None]
    k_tail = kc * jnp.exp(gc[..., -1:] - gc)[..., None]
    g_tot = jnp.exp(gc[..., -1])

    def step(S, inp):
        u_i, w_i, a_i, qd_i, kt_i, gt_i = inp
        v_new = u_i - jnp.einsum('bhcd,bhde->bhce', w_i, S)
        o_i = jnp.einsum('bhcd,bhde->bhce', qd_i, S) + jnp.einsum('bhij,bhje->bhie', a_i, v_new)
        S = S * gt_i[..., None, None] + jnp.einsum('bhcd,bhce->bhde', kt_i, v_new)
        return S, o_i

    xs = tuple(jnp.moveaxis(t, 1, 0) for t in (u_val, w_dec, attn_in, q_dec, k_tail, g_tot))
    S_fin, o = lax.scan(step, S0.astype(f32), xs)
    o = jnp.transpose(o, (1, 0, 3, 2, 4)).reshape(B, L, H, DV)
    return o.astype(v.dtype), S_fin.astype(S0.dtype)


def delta_branch(qkv_ext, z, b_raw, a_raw, S0, conv_w, a_log, dt_bias, o_gain):
    B = qkv_ext.shape[0]
    qkv = causal_short_conv(qkv_ext, conv_w)
    L = qkv.shape[1]
    q, k, v = jnp.split(qkv, [QK_WIDTH, 2 * QK_WIDTH], axis=-1)
    q = l2_normalize(q.reshape(B, L, N_HEADS, HEAD_DK)) * (HEAD_DK ** -0.5)
    k = l2_normalize(k.reshape(B, L, N_HEADS, HEAD_DK))
    v = v.reshape(B, L, N_HEADS, HEAD_DV)
    beta = jax.nn.sigmoid(b_raw.astype(jnp.float32))
    g = -jnp.exp(a_log.astype(jnp.float32)) * jax.nn.softplus(a_raw.astype(jnp.float32) + dt_bias.astype(jnp.float32))
    o, S = gated_delta_rule(q, k, v, g, beta, S0)
    of = o.astype(jnp.float32)
    of = of * lax.rsqrt(jnp.mean(of * of, -1, keepdims=True) + RMS_EPS) * o_gain.astype(jnp.float32)
    of = of * jax.nn.silu(z.astype(jnp.float32).reshape(B, L, N_HEADS, HEAD_DV))
    return of.reshape(B, L, V_WIDTH).astype(qkv_ext.dtype), S


def trunk_layer(x, pool_hist, conv_hist, S0, pos0, w_in, conv_w, a_log, dt_bias, o_gain, w_pool, pool_scale,
                w_out, ln1_g, ln1_b, w_ff1, b_ff1, w_ff2, b_ff2, ln2_g, ln2_b):
    proj = jnp.einsum('bld,de->ble', x, w_in)
    u_pool, qkv, z, ga, gb, b_raw, a_raw = jnp.split(proj, [OFF_POOL, OFF_QKV, OFF_Z, OFF_GA, OFF_GB, OFF_BETA], axis=-1)
    pool_ext = jnp.concatenate([pool_hist, u_pool], axis=1)
    conv_ext = jnp.concatenate([conv_hist, qkv], axis=1)
    y_a = pool_mixer(pool_ext, pos0, w_pool, pool_scale)
    y_b, S_new = delta_branch(conv_ext, z, b_raw, a_raw, S0, conv_w, a_log, dt_bias, o_gain)
    merged = jax.nn.sigmoid(ga) * y_a + jax.nn.sigmoid(gb) * y_b
    x = layer_norm(ALPHA * x + jnp.einsum('bld,de->ble', merged, w_out), ln1_g, ln1_b)
    h = jnp.square(jax.nn.relu(jnp.einsum('bld,df->blf', x, w_ff1) + b_ff1))
    x = layer_norm(ALPHA * x + jnp.einsum('blf,fd->bld', h, w_ff2) + b_ff2, ln2_g, ln2_b)
    return x, pool_ext[:, -POOL_STATE:], conv_ext[:, -(CONV_WIDTH - 1):], S_new


def setup_inputs(seed: int = 0) -> dict:
    key = jax.random.key(seed)
    ks = jax.random.split(key, 24)
    f32 = jnp.float32
    nrm = lambda k, shape, s: jax.random.normal(k, shape, f32) * s
    dt = jnp.exp(jax.random.uniform(ks[7], (DEPTH, N_HEADS), f32) * (math.log(0.1) - math.log(0.001)) + math.log(0.001))
    return {
        "x_prompt": nrm(ks[0], (BATCH, SEQ, D_MODEL), 1.0),
        "x_sample": nrm(ks[1], (DEC_BATCH, DEC_SEQ, D_MODEL), 1.0),
        "state_pool": nrm(ks[2], (DEPTH, DEC_BATCH, POOL_STATE, POOL_WIDTH), 1.0),
        "state_conv": nrm(ks[3], (DEPTH, DEC_BATCH, CONV_WIDTH - 1, CONV_CH), 1.0),
        "state_delta": nrm(ks[4], (DEPTH, DEC_BATCH, N_HEADS, HEAD_DK, HEAD_DV), 0.1),
        "ln_in_g": 1.0 + nrm(ks[5], (D_MODEL,), 0.02),
        "ln_in_b": nrm(ks[6], (D_MODEL,), 0.02),
        "w_in": nrm(ks[8], (DEPTH, D_MODEL, IN_WIDTH), D_MODEL ** -0.5),
        "conv_w": nrm(ks[9], (DEPTH, CONV_WIDTH, CONV_CH), CONV_WIDTH ** -0.5),
        "a_log": jnp.log(jax.random.uniform(ks[10], (DEPTH, N_HEADS), f32, 1.0, 16.0)),
        "dt_bias": dt + jnp.log(-jnp.expm1(-dt)),
        "o_gain": 1.0 + nrm(ks[11], (DEPTH, HEAD_DV), 0.02),
        "w_pool": nrm(ks[12], (DEPTH, N_POOL_GROUPS, POOL_GROUP, POOL_GROUP), POOL_GROUP ** -0.5),
        "pool_scale": 1.0 + nrm(ks[13], (DEPTH, POOL_WIDTH), 0.02),
        "w_out": nrm(ks[14], (DEPTH, D_MODEL, D_MODEL), BETA_INIT * D_MODEL ** -0.5),
        "ln1_g": 1.0 + nrm(ks[15], (DEPTH, D_MODEL), 0.02),
        "ln1_b": nrm(ks[16], (DEPTH, D_MODEL), 0.02),
        "w_ff1": nrm(ks[17], (DEPTH, D_MODEL, D_FF), D_MODEL ** -0.5),
        "b_ff1": nrm(ks[18], (DEPTH, D_FF), 0.02),
        "w_ff2": nrm(ks[19], (DEPTH, D_FF, D_MODEL), BETA_INIT * D_FF ** -0.5),
        "b_ff2": nrm(ks[20], (DEPTH, D_MODEL), 0.02),
        "ln2_g": 1.0 + nrm(ks[21], (DEPTH, D_MODEL), 0.02),
        "ln2_b": nrm(ks[22], (DEPTH, D_MODEL), 0.02),
    }


def reference(x_prompt, x_sample, state_pool, state_conv, state_delta, ln_in_g, ln_in_b, w_in, conv_w, a_log,
              dt_bias, o_gain, w_pool, pool_scale, w_out, ln1_g, ln1_b, w_ff1, b_ff1, w_ff2, b_ff2, ln2_g, ln2_b):
    B = x_prompt.shape[0]
    dt = x_prompt.dtype
    xp = layer_norm(x_prompt, ln_in_g, ln_in_b)
    xs = layer_norm(x_sample, ln_in_g, ln_in_b)
    pool_p, conv_p, delta_p, pool_s, conv_s, delta_s = [], [], [], [], [], []
    for l in range(DEPTH):
        lw = (w_in[l], conv_w[l], a_log[l], dt_bias[l], o_gain[l], w_pool[l], pool_scale[l], w_out[l],
              ln1_g[l], ln1_b[l], w_ff1[l], b_ff1[l], w_ff2[l], b_ff2[l], ln2_g[l], ln2_b[l])
        xp, pp, cp, sp = trunk_layer(xp,
                                     jnp.zeros((B, POOL_STATE, POOL_WIDTH), dt),
                                     jnp.zeros((B, CONV_WIDTH - 1, CONV_CH), dt),
                                     jnp.zeros((B, N_HEADS, HEAD_DK, HEAD_DV), dt),
                                     0, *lw)
        xs, ps, cs, ss = trunk_layer(xs, state_pool[l], state_conv[l], state_delta[l], PAST_LEN, *lw)
        pool_p.append(pp); conv_p.append(cp); delta_p.append(sp)
        pool_s.append(ps); conv_s.append(cs); delta_s.append(ss)
    return (xp, xs, jnp.stack(pool_p), jnp.stack(conv_p), jnp.stack(delta_p),
            jnp.stack(pool_s), jnp.stack(conv_s), jnp.stack(delta_s))
```

```python
import functools

import jax
import jax.numpy as jnp
from jax import lax
from jax.experimental import pallas as pl
from jax.experimental.pallas import tpu as pltpu

D_MODEL = 1024
DEPTH = 2
PAST_LEN = 1024
CHUNK = 64
POOL_WINDOWS = (2, 4, 8, 16)
N_POOL_GROUPS = 4
POOL_GROUP = D_MODEL // N_POOL_GROUPS
POOL_STATE = 15
HEAD_D = 128
N_HEADS = D_MODEL // HEAD_D
CONV_WIDTH = 4
CONV_CH = 3 * D_MODEL
D_FF = 4 * D_MODEL
ALPHA = (2 * DEPTH) ** 0.25
LN_EPS = 1e-5
RMS_EPS = 1e-6
L2_EPS = 1e-6
OFF_QKV = D_MODEL
OFF_Z = OFF_QKV + CONV_CH
OFF_GA = OFF_Z + D_MODEL
OFF_GB = OFF_GA + D_MODEL
OFF_BETA = OFF_GB + D_MODEL
MAIN_WIDTH = OFF_BETA

V7X_LANES = 128
V7X_SUBLANES = 8
V7X_VMEM_LIMIT_BYTES = 56 * 1024 * 1024

POOL_PAD = 16
CONV_PAD = 8

_BF16 = jnp.bfloat16
_F32 = jnp.float32


def _dot(a, b):
    return jnp.dot(a.astype(_BF16), b.astype(_BF16), preferred_element_type=_F32)


def _dot_nt(a, b):
    return lax.dot_general(a.astype(_BF16), b.astype(_BF16), (((1,), (1,)), ((), ())),
                           preferred_element_type=_F32)


def _dot_tn(a, b):
    return lax.dot_general(a.astype(_BF16), b.astype(_BF16), (((0,), (0,)), ((), ())),
                           preferred_element_type=_F32)


def _sigmoid(x):
    return 1.0 / (1.0 + jnp.exp(-x))


def _softplus(x):
    return jnp.maximum(x, 0.0) + jnp.log1p(jnp.exp(-jnp.abs(x)))


def _layer_norm(x, g, b):
    mu = jnp.mean(x, axis=-1, keepdims=True)
    xc = x - mu
    var = jnp.mean(xc * xc, axis=-1, keepdims=True)
    return xc * lax.rsqrt(var + LN_EPS) * g + b


def _unit_lower_inverse(a, n):
    ri = lax.broadcasted_iota(jnp.int32, (n, n), 0)
    ci = lax.broadcasted_iota(jnp.int32, (n, n), 1)
    same = lambda s: (ri >> s) == (ci >> s)
    inv = jnp.where(ri == ci, 1.0, jnp.where(same(1), -a, 0.0))
    s = 1
    while (1 << s) < n:
        a_off = jnp.where(same(s + 1) & jnp.logical_not(same(s)), a, 0.0)
        inv = inv - _dot(_dot(inv, a_off), inv)
        s += 1
    return inv


def _mixer_kernel(x_ref, ph_ref, ch_ref, s0_ref, lng_ref, lnb_ref, win_ref, wba_ref, cw_ref,
                  alog_ref, dtb_ref, og_ref, wpool_ref, pscale_ref, wout_ref, l1g_ref, l1b_ref,
                  y_ref, pool_out_ref, conv_out_ref, s_out_ref,
                  pool_buf, conv_buf, s_ref, q_ref, k_ref, v_ref, z_ref, ya_ref, yb_ref,
                  *, tile, chunk, pos0, apply_ln_in):
    t = pl.program_id(1)
    n_t = pl.num_programs(1)

    @pl.when(t == 0)
    def _():
        pool_buf[0:POOL_PAD, :] = ph_ref[0]
        conv_buf[0:CONV_PAD, :] = ch_ref[0]
        s_ref[...] = s0_ref[0]

    x = x_ref[0]
    if apply_ln_in:
        x = _layer_norm(x, lng_ref[...], lnb_ref[...])
    xb = x.astype(_BF16)

    pool_buf[POOL_PAD:POOL_PAD + tile, :] = jnp.dot(xb, win_ref[:, 0:D_MODEL],
                                                    preferred_element_type=_F32)
    for j in range(3):
        lo = OFF_QKV + j * D_MODEL
        conv_buf[CONV_PAD:CONV_PAD + tile, j * D_MODEL:(j + 1) * D_MODEL] = jnp.dot(
            xb, win_ref[:, lo:lo + D_MODEL], preferred_element_type=_F32)
    z_ref[...] = jnp.dot(xb, win_ref[:, OFF_Z:OFF_GA], preferred_element_type=_F32)

    row = lax.broadcasted_iota(jnp.int32, (tile, POOL_GROUP), 0)
    pos1 = row + (t * tile + pos0 + 1)
    u_now = pool_buf[POOL_PAD:POOL_PAD + tile, :]
    for gi, w in enumerate(POOL_WINDOWS):
        lo, hi = gi * POOL_GROUP, (gi + 1) * POOL_GROUP
        acc = u_now[:, lo:hi]
        for i in range(1, w):
            acc = acc + pool_buf[POOL_PAD - i:POOL_PAD - i + tile, lo:hi]
        cnt = jnp.minimum(pos1, w).astype(_F32)
        mixed = acc / cnt - u_now[:, lo:hi]
        ya_ref[:, lo:hi] = _dot(mixed, wpool_ref[gi]) * pscale_ref[:, lo:hi]
    new_pool = pool_buf[tile + 1:tile + POOL_PAD, :]
    pool_out_ref[0] = new_pool
    pool_buf[1:POOL_PAD, :] = new_pool

    for j in range(3 * N_HEADS):
        lo, hi = j * HEAD_D, (j + 1) * HEAD_D
        acc = conv_buf[CONV_PAD:CONV_PAD + tile, lo:hi] * cw_ref[CONV_WIDTH - 1:CONV_WIDTH, lo:hi]
        for tap in range(CONV_WIDTH - 1):
            r0 = CONV_PAD - (CONV_WIDTH - 1) + tap
            acc = acc + conv_buf[r0:r0 + tile, lo:hi] * cw_ref[tap:tap + 1, lo:hi]
        act = acc * _sigmoid(acc)
        h = j % N_HEADS
        hl, hh = h * HEAD_D, (h + 1) * HEAD_D
        if j < N_HEADS:
            nrm = lax.rsqrt(jnp.sum(act * act, axis=-1, keepdims=True) + L2_EPS)
            q_ref[:, hl:hh] = act * nrm * (HEAD_D ** -0.5)
        elif j < 2 * N_HEADS:
            nrm = lax.rsqrt(jnp.sum(act * act, axis=-1, keepdims=True) + L2_EPS)
            k_ref[:, hl:hh] = act * nrm
        else:
            v_ref[:, hl:hh] = act
    new_conv = conv_buf[CONV_PAD + tile - (CONV_WIDTH - 1):CONV_PAD + tile, :]
    conv_out_ref[0] = new_conv
    conv_buf[CONV_PAD - (CONV_WIDTH - 1):CONV_PAD, :] = new_conv

    raw = jnp.dot(xb, wba_ref[...], preferred_element_type=_F32)
    beta = _sigmoid(raw)
    g = -jnp.exp(alog_ref[...]) * _softplus(raw + dtb_ref[...])
    rowc = lax.broadcasted_iota(jnp.int32, (tile, V7X_LANES), 0) & (chunk - 1)
    gc = g
    sh = 1
    while sh < chunk:
        gc = gc + jnp.where(rowc >= sh, pltpu.roll(gc, sh, 0), 0.0)
        sh *= 2
    gc_t = gc.T

    ri = lax.broadcasted_iota(jnp.int32, (chunk, chunk), 0)
    ci = lax.broadcasted_iota(jnp.int32, (chunk, chunk), 1)
    incl = ri >= ci
    strict = ri > ci

    for h in range(N_HEADS):
        hl, hh = h * HEAD_D, (h + 1) * HEAD_D
        s_mat = s_ref[h]
        for c in range(tile // chunk):
            r0, r1 = c * chunk, (c + 1) * chunk
            qc = q_ref[r0:r1, hl:hh]
            kc = k_ref[r0:r1, hl:hh]
            vc = v_ref[r0:r1, hl:hh]
            b_col = beta[r0:r1, h:h + 1]
            g_col = gc[r0:r1, N_HEADS + h:N_HEADS + h + 1]
            g_row = gc_t[N_HEADS + h:N_HEADS + h + 1, r0:r1]
            g_last = g_col[chunk - 1:chunk, :]
            e_col = jnp.exp(g_col)
            decay = jnp.where(incl, jnp.exp(g_col - g_row), 0.0)
            kb = kc * b_col
            qk = _dot_nt(jnp.concatenate([qc, kb], axis=0), kc)
            attn = qk[0:chunk] * decay
            a_low = jnp.where(strict, qk[chunk:2 * chunk] * decay, 0.0)
            t_inv = _unit_lower_inverse(a_low, chunk)
            qs = _dot(jnp.concatenate([qc * e_col, kb * e_col], axis=0), s_mat)
            v_new = _dot(t_inv, vc * b_col - qs[chunk:2 * chunk])
            o = qs[0:chunk] + _dot(attn, v_new)
            k_tail = kc * jnp.exp(g_last - g_col)
            s_mat = s_mat * jnp.exp(g_last) + _dot_tn(k_tail, v_new)
            o = o * lax.rsqrt(jnp.mean(o * o, axis=-1, keepdims=True) + RMS_EPS) * og_ref[...]
            zc = z_ref[r0:r1, hl:hh]
            yb_ref[r0:r1, hl:hh] = o * (zc * _sigmoid(zc))
        s_ref[h] = s_mat

    @pl.when(t == n_t - 1)
    def _():
        s_out_ref[0] = s_ref[...]

    ga = jnp.dot(xb, win_ref[:, OFF_GA:OFF_GB], preferred_element_type=_F32)
    merged = _sigmoid(ga) * ya_ref[...]
    gb = jnp.dot(xb, win_ref[:, OFF_GB:OFF_BETA], preferred_element_type=_F32)
    merged = merged + _sigmoid(gb) * yb_ref[...]
    y = ALPHA * x + _dot(merged, wout_ref[...])
    y_ref[0] = _layer_norm(y, l1g_ref[...], l1b_ref[...])


def _ffn_kernel(x_ref, w1_ref, b1_ref, w2_ref, b2_ref, g_ref, b_ref, y_ref):
    x = x_ref[...]
    xb = x.astype(_BF16)
    acc = ALPHA * x + b2_ref[...]
    for j in range(D_FF // D_MODEL):
        lo, hi = j * D_MODEL, (j + 1) * D_MODEL
        h = jnp.dot(xb, w1_ref[:, lo:hi], preferred_element_type=_F32) + b1_ref[:, lo:hi]
        h = jnp.square(jnp.maximum(h, 0.0))
        acc = acc + jnp.dot(h.astype(_BF16), w2_ref[lo:hi, :], preferred_element_type=_F32)
    y_ref[...] = _layer_norm(acc, g_ref[...], b_ref[...])


def _const_spec(shape):
    return pl.BlockSpec(shape, lambda *_: (0,) * len(shape), pipeline_mode=pl.Buffered(1))


def _mixer_call(x, pool_hist, conv_hist, s0, lw, *, tile, chunk, pos0, apply_ln_in):
    nb, seq, _ = x.shape
    n_t = seq // tile
    kern = functools.partial(_mixer_kernel, tile=tile, chunk=chunk, pos0=pos0,
                             apply_ln_in=apply_ln_in)
    per_stream = lambda shape: pl.BlockSpec((1,) + shape, lambda b, t: (b,) + (0,) * len(shape))
    in_specs = [
        pl.BlockSpec((1, tile, D_MODEL), lambda b, t: (b, t, 0)),
        per_stream((POOL_PAD, D_MODEL)),
        per_stream((CONV_PAD, CONV_CH)),
        per_stream((N_HEADS, HEAD_D, HEAD_D)),
        _const_spec((1, D_MODEL)), _const_spec((1, D_MODEL)),
        _const_spec((D_MODEL, MAIN_WIDTH)),
        _const_spec((D_MODEL, V7X_LANES)),
        _const_spec((CONV_WIDTH, CONV_CH)),
        _const_spec((1, V7X_LANES)), _const_spec((1, V7X_LANES)),
        _const_spec((1, HEAD_D)),
        _const_spec((N_POOL_GROUPS, POOL_GROUP, POOL_GROUP)),
        _const_spec((1, D_MODEL)),
        _const_spec((D_MODEL, D_MODEL)),
        _const_spec((1, D_MODEL)), _const_spec((1, D_MODEL)),
    ]
    out_shape = (
        jax.ShapeDtypeStruct((nb, seq, D_MODEL), _F32),
        jax.ShapeDtypeStruct((nb, POOL_STATE, D_MODEL), _F32),
        jax.ShapeDtypeStruct((nb, CONV_WIDTH - 1, CONV_CH), _F32),
        jax.ShapeDtypeStruct((nb, N_HEADS, HEAD_D, HEAD_D), _F32),
    )
    out_specs = (
        pl.BlockSpec((1, tile, D_MODEL), lambda b, t: (b, t, 0)),
        per_stream((POOL_STATE, D_MODEL)),
        per_stream((CONV_WIDTH - 1, CONV_CH)),
        per_stream((N_HEADS, HEAD_D, HEAD_D)),
    )
    scratch = [
        pltpu.VMEM((POOL_PAD + tile, D_MODEL), _F32),
        pltpu.VMEM((CONV_PAD + tile, CONV_CH), _F32),
        pltpu.VMEM((N_HEADS, HEAD_D, HEAD_D), _F32),
        pltpu.VMEM((tile, D_MODEL), _F32),
        pltpu.VMEM((tile, D_MODEL), _F32),
        pltpu.VMEM((tile, D_MODEL), _F32),
        pltpu.VMEM((tile, D_MODEL), _F32),
        pltpu.VMEM((tile, D_MODEL), _F32),
        pltpu.VMEM((tile, D_MODEL), _F32),
    ]
    return pl.pallas_call(
        kern,
        grid=(nb, n_t),
        in_specs=in_specs,
        out_specs=out_specs,
        out_shape=out_shape,
        scratch_shapes=scratch,
        compiler_params=pltpu.CompilerParams(
            dimension_semantics=("arbitrary", "arbitrary"),
            vmem_limit_bytes=V7X_VMEM_LIMIT_BYTES),
        name="mixer",
    )(x, pool_hist, conv_hist, s0, *lw)


def _ffn_call(x, w1, b1, w2, b2, g, b, *, tile):
    n = x.shape[0]
    row_spec = pl.BlockSpec((tile, D_MODEL), lambda i: (i, 0))
    return pl.pallas_call(
        _ffn_kernel,
        grid=(n // tile,),
        in_specs=[row_spec,
                  _const_spec((D_MODEL, D_FF)), _const_spec((1, D_FF)),
                  _const_spec((D_FF, D_MODEL)), _const_spec((1, D_MODEL)),
                  _const_spec((1, D_MODEL)), _const_spec((1, D_MODEL))],
        out_specs=row_spec,
        out_shape=jax.ShapeDtypeStruct((n, D_MODEL), _F32),
        compiler_params=pltpu.CompilerParams(
            dimension_semantics=("arbitrary",),
            vmem_limit_bytes=V7X_VMEM_LIMIT_BYTES),
        name="ffn",
    )(x, w1, b1, w2, b2, g, b)


def _pad_lanes(v, offset):
    out = jnp.zeros((1, V7X_LANES), _F32)
    return lax.dynamic_update_slice(out, v.reshape(1, -1).astype(_F32), (0, offset))


def kernel(x_prompt, x_sample, state_pool, state_conv, state_delta, ln_in_g, ln_in_b, w_in, conv_w, a_log,
           dt_bias, o_gain, w_pool, pool_scale, w_out, ln1_g, ln1_b, w_ff1, b_ff1, w_ff2, b_ff2, ln2_g, ln2_b):
    nbp, seq_p, _ = x_prompt.shape
    nbs, seq_s, _ = x_sample.shape
    row = lambda v: v.reshape(1, -1).astype(_F32)

    xp, xs = x_prompt, x_sample
    outs = {k: [] for k in ("pp", "cp", "sp", "ps", "cs", "ss")}
    zeros_pool = jnp.zeros((nbp, POOL_PAD, D_MODEL), _F32)
    zeros_conv = jnp.zeros((nbp, CONV_PAD, CONV_CH), _F32)
    zeros_s = jnp.zeros((nbp, N_HEADS, HEAD_D, HEAD_D), _F32)
    for l in range(DEPTH):
        w_ba = jnp.zeros((D_MODEL, V7X_LANES), _F32)
        w_ba = lax.dynamic_update_slice(w_ba, w_in[l][:, OFF_BETA:OFF_BETA + 2 * N_HEADS], (0, 0))
        lw = (row(ln_in_g), row(ln_in_b),
              w_in[l][:, :MAIN_WIDTH].astype(_BF16), w_ba.astype(_BF16), conv_w[l],
              _pad_lanes(a_log[l], N_HEADS), _pad_lanes(dt_bias[l], N_HEADS), row(o_gain[l]),
              w_pool[l].astype(_BF16), row(pool_scale[l]), w_out[l].astype(_BF16),
              row(ln1_g[l]), row(ln1_b[l]))
        fw = (w_ff1[l].astype(_BF16), row(b_ff1[l]), w_ff2[l].astype(_BF16), row(b_ff2[l]),
              row(ln2_g[l]), row(ln2_b[l]))

        xp, pp, cp, sp = _mixer_call(xp, zeros_pool, zeros_conv, zeros_s, lw,
                                     tile=256, chunk=CHUNK, pos0=0, apply_ln_in=(l == 0))
        xp = _ffn_call(xp.reshape(nbp * seq_p, D_MODEL), *fw, tile=512).reshape(nbp, seq_p, D_MODEL)

        ph = jnp.pad(state_pool[l], ((0, 0), (POOL_PAD - POOL_STATE, 0), (0, 0)))
        ch = jnp.pad(state_conv[l], ((0, 0), (CONV_PAD - (CONV_WIDTH - 1), 0), (0, 0)))
        xs, ps, cs, ss = _mixer_call(xs, ph, ch, state_delta[l], lw,
                                     tile=seq_s, chunk=min(CHUNK, seq_s), pos0=PAST_LEN,
                                     apply_ln_in=(l == 0))
        xs = _ffn_call(xs.reshape(nbs * seq_s, D_MODEL), *fw, tile=512).reshape(nbs, seq_s, D_MODEL)

        for k, v in zip(("pp", "cp", "sp", "ps", "cs", "ss"), (pp, cp, sp, ps, cs, ss)):
            outs[k].append(v)
    return (xp, xs, jnp.stack(outs["pp"]), jnp.stack(outs["cp"]), jnp.stack(outs["sp"]),
            jnp.stack(outs["ps"]), jnp.stack(outs["cs"]), jnp.stack(outs["ss"]))
```

```python
import functools

import jax
import jax.numpy as jnp
from jax import lax
from jax.experimental import pallas as pl
from jax.experimental.pallas import tpu as pltpu

D_MODEL = 1024
DEPTH = 2
PAST_LEN = 1024
CHUNK = 64
PROMPT_CHUNK = 128
POOL_WINDOWS = (2, 4, 8, 16)
N_POOL_GROUPS = 4
POOL_GROUP = D_MODEL // N_POOL_GROUPS
POOL_STATE = 15
HEAD_D = 128
N_HEADS = D_MODEL // HEAD_D
CONV_WIDTH = 4
CONV_CH = 3 * D_MODEL
D_FF = 4 * D_MODEL
ALPHA = (2 * DEPTH) ** 0.25
LN_EPS = 1e-5
RMS_EPS = 1e-6
L2_EPS = 1e-6
OFF_QKV = D_MODEL
OFF_Z = OFF_QKV + CONV_CH
OFF_GA = OFF_Z + D_MODEL
OFF_GB = OFF_GA + D_MODEL
OFF_BETA = OFF_GB + D_MODEL
MAIN_WIDTH = OFF_BETA

V7X_LANES = 128
V7X_SUBLANES = 8
V7X_VMEM_LIMIT_BYTES = 56 * 1024 * 1024

POOL_PAD = 16
CONV_PAD = 8

_BF16 = jnp.bfloat16
_F32 = jnp.float32


def _dot(a, b):
    return jnp.dot(a.astype(_BF16), b.astype(_BF16), preferred_element_type=_F32)


def _dot_nt(a, b):
    return lax.dot_general(a.astype(_BF16), b.astype(_BF16), (((1,), (1,)), ((), ())),
                           preferred_element_type=_F32)


def _dot_tn(a, b):
    return lax.dot_general(a.astype(_BF16), b.astype(_BF16), (((0,), (0,)), ((), ())),
                           preferred_element_type=_F32)


def _sigmoid(x):
    return 1.0 / (1.0 + jnp.exp(-x))


def _softplus(x):
    return jnp.maximum(x, 0.0) + jnp.log1p(jnp.exp(-jnp.abs(x)))


def _layer_norm(x, g, b):
    mu = jnp.mean(x, axis=-1, keepdims=True)
    xc = x - mu
    var = jnp.mean(xc * xc, axis=-1, keepdims=True)
    return xc * lax.rsqrt(var + LN_EPS) * g + b


def _unit_lower_inverses(mats, n):
    ri = lax.broadcasted_iota(jnp.int32, (n, n), 0)
    ci = lax.broadcasted_iota(jnp.int32, (n, n), 1)
    same = lambda s: (ri >> s) == (ci >> s)
    eye = ri == ci
    pair = same(1)
    invs = [jnp.where(eye, 1.0, jnp.where(pair, -a, 0.0)) for a in mats]
    s = 1
    while (1 << s) < n:
        off = same(s + 1) & jnp.logical_not(same(s))
        part = [_dot(inv, jnp.where(off, a, 0.0)) for inv, a in zip(invs, mats)]
        invs = [inv - _dot(p, inv) for inv, p in zip(invs, part)]
        s += 1
    return invs


def _mixer_kernel(x_ref, ph_ref, ch_ref, s0_ref, lng_ref, lnb_ref, win_ref, wba_ref, cw_ref,
                  alog_ref, dtb_ref, og_ref, wpool_ref, pscale_ref, wout_ref, l1g_ref, l1b_ref,
                  y_ref, pool_out_ref, conv_out_ref, s_out_ref,
                  pool_buf, conv_buf, s_ref, q_ref, k_ref, v_ref, z_ref, ya_ref, yb_ref,
                  *, tile, chunk, pos0, apply_ln_in):
    t = pl.program_id(1)
    n_t = pl.num_programs(1)

    @pl.when(t == 0)
    def _():
        pool_buf[0:POOL_PAD, :] = ph_ref[0]
        conv_buf[0:CONV_PAD, :] = ch_ref[0]
        s_ref[...] = s0_ref[0]

    x = x_ref[0]
    if apply_ln_in:
        x = _layer_norm(x, lng_ref[...], lnb_ref[...])
    xb = x.astype(_BF16)

    pool_buf[POOL_PAD:POOL_PAD + tile, :] = jnp.dot(xb, win_ref[:, 0:D_MODEL],
                                                    preferred_element_type=_F32)
    for j in range(3):
        lo = OFF_QKV + j * D_MODEL
        conv_buf[CONV_PAD:CONV_PAD + tile, j * D_MODEL:(j + 1) * D_MODEL] = jnp.dot(
            xb, win_ref[:, lo:lo + D_MODEL], preferred_element_type=_F32)
    z_ref[...] = jnp.dot(xb, win_ref[:, OFF_Z:OFF_GA], preferred_element_type=_F32)

    row = lax.broadcasted_iota(jnp.int32, (tile, POOL_GROUP), 0)
    pos1 = row + (t * tile + pos0 + 1)
    u_now = pool_buf[POOL_PAD:POOL_PAD + tile, :]
    for gi, w in enumerate(POOL_WINDOWS):
        lo, hi = gi * POOL_GROUP, (gi + 1) * POOL_GROUP
        acc = u_now[:, lo:hi]
        for i in range(1, w):
            acc = acc + pool_buf[POOL_PAD - i:POOL_PAD - i + tile, lo:hi]
        cnt = jnp.minimum(pos1, w).astype(_F32)
        mixed = acc / cnt - u_now[:, lo:hi]
        ya_ref[:, lo:hi] = _dot(mixed, wpool_ref[gi]) * pscale_ref[:, lo:hi]
    new_pool = pool_buf[tile + 1:tile + POOL_PAD, :]
    pool_out_ref[0] = new_pool
    pool_buf[1:POOL_PAD, :] = new_pool

    for j in range(3 * N_HEADS):
        lo, hi = j * HEAD_D, (j + 1) * HEAD_D
        acc = conv_buf[CONV_PAD:CONV_PAD + tile, lo:hi] * cw_ref[CONV_WIDTH - 1:CONV_WIDTH, lo:hi]
        for tap in range(CONV_WIDTH - 1):
            r0 = CONV_PAD - (CONV_WIDTH - 1) + tap
            acc = acc + conv_buf[r0:r0 + tile, lo:hi] * cw_ref[tap:tap + 1, lo:hi]
        act = acc * _sigmoid(acc)
        h = j % N_HEADS
        hl, hh = h * HEAD_D, (h + 1) * HEAD_D
        if j < N_HEADS:
            nrm = lax.rsqrt(jnp.sum(act * act, axis=-1, keepdims=True) + L2_EPS)
            q_ref[:, hl:hh] = act * nrm * (HEAD_D ** -0.5)
        elif j < 2 * N_HEADS:
            nrm = lax.rsqrt(jnp.sum(act * act, axis=-1, keepdims=True) + L2_EPS)
            k_ref[:, hl:hh] = act * nrm
        else:
            v_ref[:, hl:hh] = act
    new_conv = conv_buf[CONV_PAD + tile - (CONV_WIDTH - 1):CONV_PAD + tile, :]
    conv_out_ref[0] = new_conv
    conv_buf[CONV_PAD - (CONV_WIDTH - 1):CONV_PAD, :] = new_conv

    raw = jnp.dot(xb, wba_ref[...], preferred_element_type=_F32)
    beta = _sigmoid(raw)
    g = -jnp.exp(alog_ref[...]) * _softplus(raw + dtb_ref[...])
    rowc = lax.broadcasted_iota(jnp.int32, (tile, V7X_LANES), 0) & (chunk - 1)
    gc = g
    sh = 1
    while sh < chunk:
        gc = gc + jnp.where(rowc >= sh, pltpu.roll(gc, sh, 0), 0.0)
        sh *= 2
    gc_t = gc.T

    ri = lax.broadcasted_iota(jnp.int32, (chunk, chunk), 0)
    ci = lax.broadcasted_iota(jnp.int32, (chunk, chunk), 1)
    incl = ri >= ci
    strict = ri > ci

    heads = range(N_HEADS)
    lanes = [slice(h * HEAD_D, (h + 1) * HEAD_D) for h in heads]
    s_mats = [s_ref[h] for h in heads]
    for c in range(tile // chunk):
        r0, r1 = c * chunk, (c + 1) * chunk
        kcs = [k_ref[r0:r1, lanes[h]] for h in heads]
        b_cols = [beta[r0:r1, h:h + 1] for h in heads]
        g_cols = [gc[r0:r1, N_HEADS + h:N_HEADS + h + 1] for h in heads]
        g_lasts = [g_cols[h][chunk - 1:chunk, :] for h in heads]
        e_cols = [jnp.exp(g_cols[h]) for h in heads]
        decays = [jnp.where(incl, jnp.exp(g_cols[h] - gc_t[N_HEADS + h:N_HEADS + h + 1, r0:r1]), 0.0)
                  for h in heads]
        kbs = [kcs[h] * b_cols[h] for h in heads]
        qks = [_dot_nt(jnp.concatenate([q_ref[r0:r1, lanes[h]], kbs[h]], axis=0), kcs[h])
               for h in heads]
        attns = [qks[h][0:chunk] * decays[h] for h in heads]
        t_invs = _unit_lower_inverses(
            [jnp.where(strict, qks[h][chunk:2 * chunk] * decays[h], 0.0) for h in heads], chunk)
        lhs = [jnp.concatenate([q_ref[r0:r1, lanes[h]] * e_cols[h], kbs[h] * e_cols[h]], axis=0)
               for h in heads]
        k_tails = [kcs[h] * jnp.exp(g_lasts[h] - g_cols[h]) for h in heads]
        qss = [_dot(lhs[h], s_mats[h]) for h in heads]
        v_news = [_dot(t_invs[h], v_ref[r0:r1, lanes[h]] * b_cols[h] - qss[h][chunk:2 * chunk])
                  for h in heads]
        s_mats = [s_mats[h] * jnp.exp(g_lasts[h]) + _dot_tn(k_tails[h], v_news[h]) for h in heads]
        for h in heads:
            o = qss[h][0:chunk] + _dot(attns[h], v_news[h])
            o = o * lax.rsqrt(jnp.mean(o * o, axis=-1, keepdims=True) + RMS_EPS) * og_ref[...]
            zc = z_ref[r0:r1, lanes[h]]
            yb_ref[r0:r1, lanes[h]] = o * (zc * _sigmoid(zc))
    for h in heads:
        s_ref[h] = s_mats[h]

    @pl.when(t == n_t - 1)
    def _():
        s_out_ref[0] = s_ref[...]

    ga = jnp.dot(xb, win_ref[:, OFF_GA:OFF_GB], preferred_element_type=_F32)
    merged = _sigmoid(ga) * ya_ref[...]
    gb = jnp.dot(xb, win_ref[:, OFF_GB:OFF_BETA], preferred_element_type=_F32)
    merged = merged + _sigmoid(gb) * yb_ref[...]
    y = ALPHA * x + _dot(merged, wout_ref[...])
    y_ref[0] = _layer_norm(y, l1g_ref[...], l1b_ref[...])


def _ffn_kernel(x_ref, w1_ref, b1_ref, w2_ref, b2_ref, g_ref, b_ref, y_ref):
    x = x_ref[...]
    xb = x.astype(_BF16)
    acc = ALPHA * x + b2_ref[...]
    for j in range(D_FF // D_MODEL):
        lo, hi = j * D_MODEL, (j + 1) * D_MODEL
        h = jnp.dot(xb, w1_ref[:, lo:hi], preferred_element_type=_F32) + b1_ref[:, lo:hi]
        h = jnp.square(jnp.maximum(h, 0.0))
        acc = acc + jnp.dot(h.astype(_BF16), w2_ref[lo:hi, :], preferred_element_type=_F32)
    y_ref[...] = _layer_norm(acc, g_ref[...], b_ref[...])


def _const_spec(shape):
    return pl.BlockSpec(shape, lambda *_: (0,) * len(shape), pipeline_mode=pl.Buffered(1))


def _mixer_call(x, pool_hist, conv_hist, s0, lw, *, tile, chunk, pos0, apply_ln_in):
    nb, seq, _ = x.shape
    n_t = seq // tile
    kern = functools.partial(_mixer_kernel, tile=tile, chunk=chunk, pos0=pos0,
                             apply_ln_in=apply_ln_in)
    per_stream = lambda shape: pl.BlockSpec((1,) + shape, lambda b, t: (b,) + (0,) * len(shape))
    in_specs = [
        pl.BlockSpec((1, tile, D_MODEL), lambda b, t: (b, t, 0)),
        per_stream((POOL_PAD, D_MODEL)),
        per_stream((CONV_PAD, CONV_CH)),
        per_stream((N_HEADS, HEAD_D, HEAD_D)),
        _const_spec((1, D_MODEL)), _const_spec((1, D_MODEL)),
        _const_spec((D_MODEL, MAIN_WIDTH)),
        _const_spec((D_MODEL, V7X_LANES)),
        _const_spec((CONV_WIDTH, CONV_CH)),
        _const_spec((1, V7X_LANES)), _const_spec((1, V7X_LANES)),
        _const_spec((1, HEAD_D)),
        _const_spec((N_POOL_GROUPS, POOL_GROUP, POOL_GROUP)),
        _const_spec((1, D_MODEL)),
        _const_spec((D_MODEL, D_MODEL)),
        _const_spec((1, D_MODEL)), _const_spec((1, D_MODEL)),
    ]
    out_shape = (
        jax.ShapeDtypeStruct((nb, seq, D_MODEL), _F32),
        jax.ShapeDtypeStruct((nb, POOL_STATE, D_MODEL), _F32),
        jax.ShapeDtypeStruct((nb, CONV_WIDTH - 1, CONV_CH), _F32),
        jax.ShapeDtypeStruct((nb, N_HEADS, HEAD_D, HEAD_D), _F32),
    )
    out_specs = (
        pl.BlockSpec((1, tile, D_MODEL), lambda b, t: (b, t, 0)),
        per_stream((POOL_STATE, D_MODEL)),
        per_stream((CONV_WIDTH - 1, CONV_CH)),
        per_stream((N_HEADS, HEAD_D, HEAD_D)),
    )
    scratch = [
        pltpu.VMEM((POOL_PAD + tile, D_MODEL), _F32),
        pltpu.VMEM((CONV_PAD + tile, CONV_CH), _F32),
        pltpu.VMEM((N_HEADS, HEAD_D, HEAD_D), _F32),
        pltpu.VMEM((tile, D_MODEL), _F32),
        pltpu.VMEM((tile, D_MODEL), _F32),
        pltpu.VMEM((tile, D_MODEL), _F32),
        pltpu.VMEM((tile, D_MODEL), _F32),
        pltpu.VMEM((tile, D_MODEL), _F32),
        pltpu.VMEM((tile, D_MODEL), _F32),
    ]
    return pl.pallas_call(
        kern,
        grid=(nb, n_t),
        in_specs=in_specs,
        out_specs=out_specs,
        out_shape=out_shape,
        scratch_shapes=scratch,
        compiler_params=pltpu.CompilerParams(
            dimension_semantics=("arbitrary", "arbitrary"),
            vmem_limit_bytes=V7X_VMEM_LIMIT_BYTES),
        name="mixer",
    )(x, pool_hist, conv_hist, s0, *lw)


def _ffn_call(x, w1, b1, w2, b2, g, b, *, tile):
    n = x.shape[0]
    row_spec = pl.BlockSpec((tile, D_MODEL), lambda i: (i, 0))
    return pl.pallas_call(
        _ffn_kernel,
        grid=(n // tile,),
        in_specs=[row_spec,
                  _const_spec((D_MODEL, D_FF)), _const_spec((1, D_FF)),
                  _const_spec((D_FF, D_MODEL)), _const_spec((1, D_MODEL)),
                  _const_spec((1, D_MODEL)), _const_spec((1, D_MODEL))],
        out_specs=row_spec,
        out_shape=jax.ShapeDtypeStruct((n, D_MODEL), _F32),
        compiler_params=pltpu.CompilerParams(
            dimension_semantics=("arbitrary",),
            vmem_limit_bytes=V7X_VMEM_LIMIT_BYTES),
        name="ffn",
    )(x, w1, b1, w2, b2, g, b)


def _pad_lanes(v, offset):
    out = jnp.zeros((1, V7X_LANES), _F32)
    return lax.dynamic_update_slice(out, v.reshape(1, -1).astype(_F32), (0, offset))


def kernel(x_prompt, x_sample, state_pool, state_conv, state_delta, ln_in_g, ln_in_b, w_in, conv_w, a_log,
           dt_bias, o_gain, w_pool, pool_scale, w_out, ln1_g, ln1_b, w_ff1, b_ff1, w_ff2, b_ff2, ln2_g, ln2_b):
    nbp, seq_p, _ = x_prompt.shape
    nbs, seq_s, _ = x_sample.shape
    row = lambda v: v.reshape(1, -1).astype(_F32)

    xp, xs = x_prompt, x_sample
    outs = {k: [] for k in ("pp", "cp", "sp", "ps", "cs", "ss")}
    zeros_pool = jnp.zeros((nbp, POOL_PAD, D_MODEL), _F32)
    zeros_conv = jnp.zeros((nbp, CONV_PAD, CONV_CH), _F32)
    zeros_s = jnp.zeros((nbp, N_HEADS, HEAD_D, HEAD_D), _F32)
    for l in range(DEPTH):
        w_ba = jnp.zeros((D_MODEL, V7X_LANES), _F32)
        w_ba = lax.dynamic_update_slice(w_ba, w_in[l][:, OFF_BETA:OFF_BETA + 2 * N_HEADS], (0, 0))
        lw = (row(ln_in_g), row(ln_in_b),
              w_in[l][:, :MAIN_WIDTH].astype(_BF16), w_ba.astype(_BF16), conv_w[l],
              _pad_lanes(a_log[l], N_HEADS), _pad_lanes(dt_bias[l], N_HEADS), row(o_gain[l]),
              w_pool[l].astype(_BF16), row(pool_scale[l]), w_out[l].astype(_BF16),
              row(ln1_g[l]), row(ln1_b[l]))
        fw = (w_ff1[l].astype(_BF16), row(b_ff1[l]), w_ff2[l].astype(_BF16), row(b_ff2[l]),
              row(ln2_g[l]), row(ln2_b[l]))

        xp, pp, cp, sp = _mixer_call(xp, zeros_pool, zeros_conv, zeros_s, lw,
                                     tile=256, chunk=PROMPT_CHUNK, pos0=0, apply_ln_in=(l == 0))
        xp = _ffn_call(xp.reshape(nbp * seq_p, D_MODEL), *fw, tile=512).reshape(nbp, seq_p, D_MODEL)

        ph = jnp.pad(state_pool[l], ((0, 0), (POOL_PAD - POOL_STATE, 0), (0, 0)))
        ch = jnp.pad(state_conv[l], ((0, 0), (CONV_PAD - (CONV_WIDTH - 1), 0), (0, 0)))
        xs, ps, cs, ss = _mixer_call(xs, ph, ch, state_delta[l], lw,
                                     tile=seq_s, chunk=min(CHUNK, seq_s), pos0=PAST_LEN,
                                     apply_ln_in=(l == 0))
        xs = _ffn_call(xs.reshape(nbs * seq_s, D_MODEL), *fw, tile=512).reshape(nbs, seq_s, D_MODEL)

        for k, v in zip(("pp", "cp", "sp", "ps", "cs", "ss"), (pp, cp, sp, ps, cs, ss)):
            outs[k].append(v)
    return (xp, xs, jnp.stack(outs["pp"]), jnp.stack(outs["cp"]), jnp.stack(outs["sp"]),
            jnp.stack(outs["ps"]), jnp.stack(outs["cs"]), jnp.stack(outs["ss"]))
```

```python
import functools
import math

import jax
import jax.numpy as jnp
from jax import lax
from jax.experimental import pallas as pl
from jax.experimental.pallas import tpu as pltpu

D_MODEL = 1024
DEPTH = 2
PAST_LEN = 1024
CHUNK = 64
PROMPT_CHUNK = 128
POOL_WINDOWS = (2, 4, 8, 16)
N_POOL_GROUPS = 4
POOL_GROUP = D_MODEL // N_POOL_GROUPS
POOL_STATE = 15
HEAD_D = 128
N_HEADS = D_MODEL // HEAD_D
CONV_WIDTH = 4
CONV_CH = 3 * D_MODEL
D_FF = 4 * D_MODEL
ALPHA = (2 * DEPTH) ** 0.25
LN_EPS = 1e-5
RMS_EPS = 1e-6
L2_EPS = 1e-6
OFF_QKV = D_MODEL
OFF_Z = OFF_QKV + CONV_CH
OFF_GA = OFF_Z + D_MODEL
OFF_GB = OFF_GA + D_MODEL
OFF_BETA = OFF_GB + D_MODEL
MAIN_WIDTH = OFF_BETA
NEG_LOG2_E = -1.0 / math.log(2.0)

V7X_LANES = 128
V7X_SUBLANES = 8
V7X_VMEM_LIMIT_BYTES = 60 * 1024 * 1024

POOL_PAD = 16
CONV_PAD = 8
ROW_BLOCK = 128
MIXER_TILE = 512
FFN_TILE = 512

_BF16 = jnp.bfloat16
_F32 = jnp.float32


def _dot(a, b):
    return jnp.dot(a.astype(_BF16), b.astype(_BF16), preferred_element_type=_F32)


def _dot_nt(a, b):
    return lax.dot_general(a.astype(_BF16), b.astype(_BF16), (((1,), (1,)), ((), ())),
                           preferred_element_type=_F32)


def _dot_tn(a, b):
    return lax.dot_general(a.astype(_BF16), b.astype(_BF16), (((0,), (0,)), ((), ())),
                           preferred_element_type=_F32)


def _sigmoid(x):
    return 1.0 / (1.0 + jnp.exp2(x * NEG_LOG2_E))


def _softplus(x):
    return jnp.maximum(x, 0.0) + jnp.log1p(jnp.exp(-jnp.abs(x)))


def _layer_norm(x, g, b):
    mu = jnp.mean(x, axis=-1, keepdims=True)
    xc = x - mu
    var = jnp.mean(xc * xc, axis=-1, keepdims=True)
    return xc * lax.rsqrt(var + LN_EPS) * g + b


def _unit_lower_inverses(mats, n):
    ri = lax.broadcasted_iota(jnp.int32, (n, n), 0)
    ci = lax.broadcasted_iota(jnp.int32, (n, n), 1)
    same = lambda s: (ri >> s) == (ci >> s)
    as_mask = lambda cond: jnp.where(cond, 1.0, 0.0).astype(_BF16)
    eye = as_mask(ri == ci)
    pair = as_mask(same(1))
    invs = [eye - a * pair for a in mats]
    s = 1
    while (1 << s) < n:
        off = as_mask(same(s + 1) & jnp.logical_not(same(s)))
        part = [jnp.dot(inv, a * off, preferred_element_type=_F32) for inv, a in zip(invs, mats)]
        invs = [inv - jnp.dot(p.astype(_BF16), inv, preferred_element_type=_F32).astype(_BF16)
                for inv, p in zip(invs, part)]
        s += 1
    return invs


def _mixer_kernel(x_ref, ph_ref, ch_ref, s0_ref, lng_ref, lnb_ref, win_ref, wba_ref, cw_ref,
                  alog_ref, dtb_ref, og_ref, wpool_ref, pscale_ref, wout_ref, l1g_ref, l1b_ref,
                  y_ref, pool_out_ref, conv_out_ref, s_out_ref,
                  pool_buf, conv_buf, s_ref, zy_ref,
                  *, tile, chunk, pos0, apply_ln_in):
    t = pl.program_id(1)
    n_t = pl.num_programs(1)
    blk = min(tile, ROW_BLOCK)
    n_blk = tile // blk

    @pl.when(t == 0)
    def _():
        pool_buf[0:POOL_PAD, :] = ph_ref[0]
        conv_buf[0:CONV_PAD, :] = ch_ref[0]
        s_ref[...] = s0_ref[0]

    x = x_ref[0]
    if apply_ln_in:
        x = _layer_norm(x, lng_ref[...], lnb_ref[...])
        y_ref[0] = x
    res_ref = y_ref if apply_ln_in else x_ref
    xb = x.astype(_BF16)

    raw = jnp.dot(xb, wba_ref[...], preferred_element_type=_F32)
    beta = _sigmoid(raw)
    g = -jnp.exp(alog_ref[...]) * _softplus(raw + dtb_ref[...])
    rowc = lax.broadcasted_iota(jnp.int32, (tile, V7X_LANES), 0) & (chunk - 1)
    gc = g
    sh = 1
    while sh < chunk:
        gc = gc + jnp.where(rowc >= sh, pltpu.roll(gc, sh, 0), 0.0)
        sh *= 2
    gc_t = gc.T

    row = lax.broadcasted_iota(jnp.int32, (blk, V7X_LANES), 0)

    def pool_block(bi):
        r0 = bi * blk
        pos1 = row + (t * tile + (pos0 + 1 + r0))
        for gi, w in enumerate(POOL_WINDOWS):
            cnt = jnp.minimum(pos1, w).astype(_F32)
            for lo in range(gi * POOL_GROUP, (gi + 1) * POOL_GROUP, V7X_LANES):
                ext = pool_buf[r0:r0 + POOL_PAD + blk, lo:lo + V7X_LANES]
                acc = ext
                sh = 1
                while sh < w:
                    acc = acc + pltpu.roll(acc, sh, 0)
                    sh *= 2
                pool_buf[r0 + POOL_PAD:r0 + POOL_PAD + blk, lo:lo + V7X_LANES] = (
                    acc[POOL_PAD:] / cnt - ext[POOL_PAD:])

    def pool_matmuls():
        for gi in range(N_POOL_GROUPS):
            lo, hi = gi * POOL_GROUP, (gi + 1) * POOL_GROUP
            mixed = pool_buf[POOL_PAD:POOL_PAD + tile, lo:hi]
            pool_buf[POOL_PAD:POOL_PAD + tile, lo:hi] = (
                _dot(mixed, wpool_ref[gi]) * pscale_ref[:, lo:hi])

    def conv_block(part, bi):
        r0 = bi * blk
        for j in range(part * N_HEADS, (part + 1) * N_HEADS):
            lo, hi = j * HEAD_D, (j + 1) * HEAD_D
            ext = conv_buf[r0:r0 + CONV_PAD + blk, lo:hi]
            acc = ext[CONV_PAD:] * cw_ref[CONV_WIDTH - 1:CONV_WIDTH, lo:hi]
            for d in range(1, CONV_WIDTH):
                tap = CONV_WIDTH - 1 - d
                acc = acc + pltpu.roll(ext, d, 0)[CONV_PAD:] * cw_ref[tap:tap + 1, lo:hi]
            act = acc * _sigmoid(acc)
            if part < 2:
                scale = lax.rsqrt(jnp.sum(act * act, axis=-1, keepdims=True) + L2_EPS)
                if part == 0:
                    scale = scale * (HEAD_D ** -0.5)
                act = act * scale
            conv_buf[r0 + CONV_PAD:r0 + CONV_PAD + blk, lo:hi] = act

    pool_buf[POOL_PAD:POOL_PAD + tile, :] = jnp.dot(xb, win_ref[:, 0:D_MODEL],
                                                    preferred_element_type=_F32)
    for part in range(3):
        lo = OFF_QKV + part * D_MODEL
        conv_buf[CONV_PAD:CONV_PAD + tile, part * D_MODEL:(part + 1) * D_MODEL] = jnp.dot(
            xb, win_ref[:, lo:lo + D_MODEL], preferred_element_type=_F32)
    zy_ref[...] = jnp.dot(xb, win_ref[:, OFF_Z:OFF_GA], preferred_element_type=_F32)

    new_pool = pool_buf[tile + 1:tile + POOL_PAD, :]
    pool_out_ref[0] = new_pool
    for bi in reversed(range(n_blk)):
        pool_block(bi)
    pool_matmuls()
    pool_buf[1:POOL_PAD, :] = new_pool

    new_conv = conv_buf[CONV_PAD + tile - (CONV_WIDTH - 1):CONV_PAD + tile, :]
    conv_out_ref[0] = new_conv
    for bi in reversed(range(n_blk)):
        for part in range(3):
            conv_block(part, bi)
    conv_buf[CONV_PAD - (CONV_WIDTH - 1):CONV_PAD, :] = new_conv

    ri = lax.broadcasted_iota(jnp.int32, (chunk, chunk), 0)
    ci = lax.broadcasted_iota(jnp.int32, (chunk, chunk), 1)
    incl = ri >= ci
    strict = ri > ci

    heads = range(N_HEADS)
    q_lanes = [slice(h * HEAD_D, (h + 1) * HEAD_D) for h in heads]
    k_lanes = [slice(D_MODEL + h * HEAD_D, D_MODEL + (h + 1) * HEAD_D) for h in heads]
    v_lanes = [slice(2 * D_MODEL + h * HEAD_D, 2 * D_MODEL + (h + 1) * HEAD_D) for h in heads]
    n_chunks = tile // chunk
    units = [(c, h) for c in range(n_chunks) for h in heads]

    crow = lambda c: slice(CONV_PAD + c * chunk, CONV_PAD + (c + 1) * chunk)
    trow = lambda c: slice(c * chunk, (c + 1) * chunk)
    gc_cs = [gc[trow(c), :] for c in range(n_chunks)]
    gc_lasts = [g[chunk - 1:chunk, :] for g in gc_cs]
    e_alls = [jnp.exp(g) for g in gc_cs]
    tail_alls = [jnp.exp(gl - g) for gl, g in zip(gc_lasts, gc_cs)]
    carry_alls = [jnp.exp(gl) for gl in gc_lasts]
    col = lambda arr, h: arr[:, N_HEADS + h:N_HEADS + h + 1]
    kbs = {u: conv_buf[crow(u[0]), k_lanes[u[1]]] * beta[trow(u[0]), u[1]:u[1] + 1] for u in units}
    decays = {(c, h): jnp.where(
        incl, jnp.exp(col(gc_cs[c], h) - gc_t[N_HEADS + h:N_HEADS + h + 1, trow(c)]), 0.0)
        for c, h in units}
    qks = {(c, h): _dot_nt(jnp.concatenate([conv_buf[crow(c), q_lanes[h]], kbs[c, h]], axis=0),
                           conv_buf[crow(c), k_lanes[h]]) for c, h in units}
    attns = {u: (qks[u][0:chunk] * decays[u]).astype(_BF16) for u in units}
    t_invs = dict(zip(units, _unit_lower_inverses(
        [jnp.where(strict, qks[u][chunk:2 * chunk] * decays[u], 0.0).astype(_BF16) for u in units],
        chunk)))

    s_mats = [s_ref[h] for h in heads]
    for c in range(n_chunks):
        r0, r1 = c * chunk, (c + 1) * chunk
        e_cols = [col(e_alls[c], h) for h in heads]
        lhs = [jnp.concatenate([conv_buf[crow(c), q_lanes[h]] * e_cols[h], kbs[c, h] * e_cols[h]],
                               axis=0) for h in heads]
        k_tails = [conv_buf[crow(c), k_lanes[h]] * col(tail_alls[c], h) for h in heads]
        qss = [_dot(lhs[h], s_mats[h]) for h in heads]
        v_news = [_dot(t_invs[c, h], conv_buf[crow(c), v_lanes[h]] * beta[r0:r1, h:h + 1]
                       - qss[h][chunk:2 * chunk]) for h in heads]
        s_mats = [s_mats[h] * col(carry_alls[c], h) + _dot_tn(k_tails[h], v_news[h]) for h in heads]
        for h in heads:
            o = qss[h][0:chunk] + _dot(attns[c, h], v_news[h])
            o = o * lax.rsqrt(jnp.mean(o * o, axis=-1, keepdims=True) + RMS_EPS) * og_ref[...]
            zc = zy_ref[r0:r1, q_lanes[h]]
            zy_ref[r0:r1, q_lanes[h]] = o * (zc * _sigmoid(zc))
    for h in heads:
        s_ref[h] = s_mats[h]

    ga = jnp.dot(xb, win_ref[:, OFF_GA:OFF_GB], preferred_element_type=_F32)
    merged = _sigmoid(ga) * pool_buf[POOL_PAD:POOL_PAD + tile, :]
    gb = jnp.dot(xb, win_ref[:, OFF_GB:OFF_BETA], preferred_element_type=_F32)
    merged = merged + _sigmoid(gb) * zy_ref[...]
    y = ALPHA * res_ref[0] + _dot(merged, wout_ref[...])
    y_ref[0] = _layer_norm(y, l1g_ref[...], l1b_ref[...])

    @pl.when(t == n_t - 1)
    def _():
        s_out_ref[0] = s_ref[...]


def _ffn_kernel(x_ref, w1_ref, b1_ref, w2_ref, b2_ref, g_ref, b_ref, y_ref):
    x = x_ref[...]
    xb = x.astype(_BF16)
    acc = ALPHA * x + b2_ref[...]
    for j in range(D_FF // D_MODEL):
        lo, hi = j * D_MODEL, (j + 1) * D_MODEL
        h = jnp.dot(xb, w1_ref[:, lo:hi], preferred_element_type=_F32) + b1_ref[:, lo:hi]
        h = jnp.square(jnp.maximum(h, 0.0))
        acc = acc + jnp.dot(h.astype(_BF16), w2_ref[lo:hi, :], preferred_element_type=_F32)
    y_ref[...] = _layer_norm(acc, g_ref[...], b_ref[...])


def _const_spec(shape):
    return pl.BlockSpec(shape, lambda *_: (0,) * len(shape), pipeline_mode=pl.Buffered(1))


def _mixer_call(x, pool_hist, conv_hist, s0, lw, *, chunk, pos0, apply_ln_in):
    nb, seq, _ = x.shape
    tile = min(seq, MIXER_TILE)
    n_t = seq // tile
    kern = functools.partial(_mixer_kernel, tile=tile, chunk=chunk, pos0=pos0,
                             apply_ln_in=apply_ln_in)
    per_stream = lambda shape: pl.BlockSpec((1,) + shape, lambda b, t: (b,) + (0,) * len(shape))
    in_specs = [
        pl.BlockSpec((1, tile, D_MODEL), lambda b, t: (b, t, 0)),
        per_stream((POOL_PAD, D_MODEL)),
        per_stream((CONV_PAD, CONV_CH)),
        per_stream((N_HEADS, HEAD_D, HEAD_D)),
        _const_spec((1, D_MODEL)), _const_spec((1, D_MODEL)),
        _const_spec((D_MODEL, MAIN_WIDTH)),
        _const_spec((D_MODEL, V7X_LANES)),
        _const_spec((CONV_WIDTH, CONV_CH)),
        _const_spec((1, V7X_LANES)), _const_spec((1, V7X_LANES)),
        _const_spec((1, HEAD_D)),
        _const_spec((N_POOL_GROUPS, POOL_GROUP, POOL_GROUP)),
        _const_spec((1, D_MODEL)),
        _const_spec((D_MODEL, D_MODEL)),
        _const_spec((1, D_MODEL)), _const_spec((1, D_MODEL)),
    ]
    out_shape = (
        jax.ShapeDtypeStruct((nb, seq, D_MODEL), _F32),
        jax.ShapeDtypeStruct((nb, POOL_STATE, D_MODEL), _F32),
        jax.ShapeDtypeStruct((nb, CONV_WIDTH - 1, CONV_CH), _F32),
        jax.ShapeDtypeStruct((nb, N_HEADS, HEAD_D, HEAD_D), _F32),
    )
    out_specs = (
        pl.BlockSpec((1, tile, D_MODEL), lambda b, t: (b, t, 0)),
        per_stream((POOL_STATE, D_MODEL)),
        per_stream((CONV_WIDTH - 1, CONV_CH)),
        per_stream((N_HEADS, HEAD_D, HEAD_D)),
    )
    scratch = [
        pltpu.VMEM((POOL_PAD + tile, D_MODEL), _F32),
        pltpu.VMEM((CONV_PAD + tile, CONV_CH), _F32),
        pltpu.VMEM((N_HEADS, HEAD_D, HEAD_D), _F32),
        pltpu.VMEM((tile, D_MODEL), _F32),
    ]
    return pl.pallas_call(
        kern,
        grid=(nb, n_t),
        in_specs=in_specs,
        out_specs=out_specs,
        out_shape=out_shape,
        scratch_shapes=scratch,
        compiler_params=pltpu.CompilerParams(
            dimension_semantics=("arbitrary", "arbitrary"),
            vmem_limit_bytes=V7X_VMEM_LIMIT_BYTES),
        name="mixer",
    )(x, pool_hist, conv_hist, s0, *lw)


def _ffn_call(x, w1, b1, w2, b2, g, b):
    n = x.shape[0]
    row_spec = pl.BlockSpec((FFN_TILE, D_MODEL), lambda i: (i, 0))
    return pl.pallas_call(
        _ffn_kernel,
        grid=(n // FFN_TILE,),
        in_specs=[row_spec,
                  _const_spec((D_MODEL, D_FF)), _const_spec((1, D_FF)),
                  _const_spec((D_FF, D_MODEL)), _const_spec((1, D_MODEL)),
                  _const_spec((1, D_MODEL)), _const_spec((1, D_MODEL))],
        out_specs=row_spec,
        out_shape=jax.ShapeDtypeStruct((n, D_MODEL), _F32),
        compiler_params=pltpu.CompilerParams(
            dimension_semantics=("arbitrary",),
            vmem_limit_bytes=V7X_VMEM_LIMIT_BYTES),
        name="ffn",
    )(x, w1, b1, w2, b2, g, b)


def _pad_lanes(v, offset):
    out = jnp.zeros((1, V7X_LANES), _F32)
    return lax.dynamic_update_slice(out, v.reshape(1, -1).astype(_F32), (0, offset))


def kernel(x_prompt, x_sample, state_pool, state_conv, state_delta, ln_in_g, ln_in_b, w_in, conv_w, a_log,
           dt_bias, o_gain, w_pool, pool_scale, w_out, ln1_g, ln1_b, w_ff1, b_ff1, w_ff2, b_ff2, ln2_g, ln2_b):
    nbp, seq_p, _ = x_prompt.shape
    nbs, seq_s, _ = x_sample.shape
    row = lambda v: v.reshape(1, -1).astype(_F32)

    xp, xs = x_prompt, x_sample
    outs = {k: [] for k in ("pp", "cp", "sp", "ps", "cs", "ss")}
    zeros_pool = jnp.zeros((nbp, POOL_PAD, D_MODEL), _F32)
    zeros_conv = jnp.zeros((nbp, CONV_PAD, CONV_CH), _F32)
    zeros_s = jnp.zeros((nbp, N_HEADS, HEAD_D, HEAD_D), _F32)
    for l in range(DEPTH):
        w_ba = jnp.zeros((D_MODEL, V7X_LANES), _F32)
        w_ba = lax.dynamic_update_slice(w_ba, w_in[l][:, OFF_BETA:OFF_BETA + 2 * N_HEADS], (0, 0))
        lw = (row(ln_in_g), row(ln_in_b),
              w_in[l][:, :MAIN_WIDTH].astype(_BF16), w_ba.astype(_BF16), conv_w[l],
              _pad_lanes(a_log[l], N_HEADS), _pad_lanes(dt_bias[l], N_HEADS), row(o_gain[l]),
              w_pool[l].astype(_BF16), row(pool_scale[l]), w_out[l].astype(_BF16),
              row(ln1_g[l]), row(ln1_b[l]))
        fw = (w_ff1[l].astype(_BF16), row(b_ff1[l]), w_ff2[l].astype(_BF16), row(b_ff2[l]),
              row(ln2_g[l]), row(ln2_b[l]))

        xp, pp, cp, sp = _mixer_call(xp, zeros_pool, zeros_conv, zeros_s, lw,
                                     chunk=PROMPT_CHUNK, pos0=0, apply_ln_in=(l == 0))
        xp = _ffn_call(xp.reshape(nbp * seq_p, D_MODEL), *fw).reshape(nbp, seq_p, D_MODEL)

        ph = jnp.pad(state_pool[l], ((0, 0), (POOL_PAD - POOL_STATE, 0), (0, 0)))
        ch = jnp.pad(state_conv[l], ((0, 0), (CONV_PAD - (CONV_WIDTH - 1), 0), (0, 0)))
        xs, ps, cs, ss = _mixer_call(xs, ph, ch, state_delta[l], lw,
                                     chunk=min(CHUNK, seq_s), pos0=PAST_LEN, apply_ln_in=(l == 0))
        xs = _ffn_call(xs.reshape(nbs * seq_s, D_MODEL), *fw).reshape(nbs, seq_s, D_MODEL)

        for k, v in zip(("pp", "cp", "sp", "ps", "cs", "ss"), (pp, cp, sp, ps, cs, ss)):
            outs[k].append(v)
    return (xp, xs, jnp.stack(outs["pp"]), jnp.stack(outs["cp"]), jnp.stack(outs["sp"]),
            jnp.stack(outs["ps"]), jnp.stack(outs["cs"]), jnp.stack(outs["ss"]))
```

```python
import functools
import math

import jax
import jax.numpy as jnp
from jax import lax
from jax.experimental import pallas as pl
from jax.experimental.pallas import tpu as pltpu

D_MODEL = 1024
DEPTH = 2
PAST_LEN = 1024
CHUNK = 64
PROMPT_CHUNK = 128
POOL_WINDOWS = (2, 4, 8, 16)
N_POOL_GROUPS = 4
POOL_GROUP = D_MODEL // N_POOL_GROUPS
POOL_STATE = 15
HEAD_D = 128
N_HEADS = D_MODEL // HEAD_D
CONV_WIDTH = 4
CONV_CH = 3 * D_MODEL
D_FF = 4 * D_MODEL
ALPHA = (2 * DEPTH) ** 0.25
LN_EPS = 1e-5
RMS_EPS = 1e-6
L2_EPS = 1e-6
OFF_QKV = D_MODEL
OFF_Z = OFF_QKV + CONV_CH
OFF_GA = OFF_Z + D_MODEL
OFF_GB = OFF_GA + D_MODEL
OFF_BETA = OFF_GB + D_MODEL
MAIN_WIDTH = OFF_BETA
NEG_LOG2_E = -1.0 / math.log(2.0)

V7X_LANES = 128
V7X_SUBLANES = 8
BF16_ROWS_PER_TILE = 2 * V7X_SUBLANES
V7X_VMEM_LIMIT_BYTES = 60 * 1024 * 1024
WIDE_BLOCK = BF16_ROWS_PER_TILE

POOL_PAD = 16
CONV_PAD = 8
ROW_BLOCK = 128
MIXER_TILE = 512
FFN_TILE = 512

_BF16 = jnp.bfloat16
_F32 = jnp.float32


def _dot(a, b):
    return jnp.dot(a.astype(_BF16), b.astype(_BF16), preferred_element_type=_F32)


def _dot_nt(a, b):
    return lax.dot_general(a.astype(_BF16), b.astype(_BF16), (((1,), (1,)), ((), ())),
                           preferred_element_type=_F32)


def _dot_tn(a, b):
    return lax.dot_general(a.astype(_BF16), b.astype(_BF16), (((0,), (0,)), ((), ())),
                           preferred_element_type=_F32)


def _sigmoid(x):
    return 1.0 / (1.0 + jnp.exp2(x * NEG_LOG2_E))


def _softplus(x):
    return jnp.maximum(x, 0.0) + jnp.log1p(jnp.exp(-jnp.abs(x)))


def _layer_norm(x, g, b):
    mu = jnp.mean(x, axis=-1, keepdims=True)
    xc = x - mu
    var = jnp.mean(xc * xc, axis=-1, keepdims=True)
    return xc * lax.rsqrt(var + LN_EPS) * g + b


def _unit_lower_inverses(mats, n):
    as_mask = lambda cond: jnp.where(cond, 1.0, 0.0).astype(_BF16)
    mm = lambda a, b: jnp.dot(a, b, preferred_element_type=_F32)
    ri = lax.broadcasted_iota(jnp.int32, (n, n), 0)
    ci = lax.broadcasted_iota(jnp.int32, (n, n), 1)
    same = lambda s: (ri >> s) == (ci >> s)
    s = 1
    if n == V7X_LANES:
        wb = WIDE_BLOCK
        wi = lax.broadcasted_iota(jnp.int32, (wb, n), 0)
        wj = lax.broadcasted_iota(jnp.int32, (wb, n), 1) & (wb - 1)
        wsame = lambda lvl: (wi >> lvl) == (wj >> lvl)
        diag_blocks = as_mask(same(int(math.log2(wb))))

        def expand(w):
            return jnp.concatenate([w] * (n // wb), axis=0) * diag_blocks

        def gather(a):
            masked = a * diag_blocks
            out = masked[0:wb]
            for b in range(1, n // wb):
                out = out + masked[b * wb:(b + 1) * wb]
            return out

        wides = [gather(a) for a in mats]
        winvs = [as_mask(wi == wj) - w * as_mask(wsame(1)) for w in wides]
        while (1 << s) < wb:
            off = as_mask(wsame(s + 1) & jnp.logical_not(wsame(s)))
            part = [mm(winv, expand(w * off)) for winv, w in zip(winvs, wides)]
            winvs = [winv - mm(p.astype(_BF16), expand(winv)).astype(_BF16)
                     for winv, p in zip(winvs, part)]
            s += 1
        invs = [expand(winv) for winv in winvs]
    else:
        invs = [as_mask(ri == ci) - a * as_mask(same(1)) for a in mats]
    while (1 << s) < n:
        off = as_mask(same(s + 1) & jnp.logical_not(same(s)))
        part = [mm(inv, a * off) for inv, a in zip(invs, mats)]
        invs = [inv - mm(p.astype(_BF16), inv).astype(_BF16) for inv, p in zip(invs, part)]
        s += 1
    return invs


def _mixer_kernel(x_ref, ph_ref, ch_ref, s0_ref, lng_ref, lnb_ref, win_ref, wba_ref, cw_ref,
                  alog_ref, dtb_ref, og_ref, wpool_ref, pscale_ref, wout_ref, l1g_ref, l1b_ref,
                  y_ref, pool_out_ref, conv_out_ref, s_out_ref,
                  pool_buf, conv_buf, s_ref, zy_ref,
                  *, tile, chunk, pos0, apply_ln_in):
    t = pl.program_id(1)
    n_t = pl.num_programs(1)
    blk = min(tile, ROW_BLOCK)
    n_blk = tile // blk

    @pl.when(t == 0)
    def _():
        pool_buf[0:POOL_PAD, :] = ph_ref[0]
        conv_buf[0:CONV_PAD, :] = ch_ref[0]
        s_ref[...] = s0_ref[0]

    x = x_ref[0]
    if apply_ln_in:
        x = _layer_norm(x, lng_ref[...], lnb_ref[...])
        y_ref[0] = x
    res_ref = y_ref if apply_ln_in else x_ref
    xb = x.astype(_BF16)

    raw = jnp.dot(xb, wba_ref[...], preferred_element_type=_F32)
    beta = _sigmoid(raw)
    g = -jnp.exp(alog_ref[...]) * _softplus(raw + dtb_ref[...])
    rowc = lax.broadcasted_iota(jnp.int32, (tile, V7X_LANES), 0) & (chunk - 1)
    gc = g
    sh = 1
    while sh < chunk:
        gc = gc + jnp.where(rowc >= sh, pltpu.roll(gc, sh, 0), 0.0)
        sh *= 2
    gc_t = gc.T

    row = lax.broadcasted_iota(jnp.int32, (blk, V7X_LANES), 0)

    def pool_block(bi):
        r0 = bi * blk
        pos1 = row + (t * tile + (pos0 + 1 + r0))
        for gi, w in enumerate(POOL_WINDOWS):
            cnt = jnp.minimum(pos1, w).astype(_F32)
            for lo in range(gi * POOL_GROUP, (gi + 1) * POOL_GROUP, V7X_LANES):
                ext = pool_buf[r0:r0 + POOL_PAD + blk, lo:lo + V7X_LANES]
                acc = ext
                sh = 1
                while sh < w:
                    acc = acc + pltpu.roll(acc, sh, 0)
                    sh *= 2
                pool_buf[r0 + POOL_PAD:r0 + POOL_PAD + blk, lo:lo + V7X_LANES] = (
                    acc[POOL_PAD:] / cnt - ext[POOL_PAD:])

    def pool_matmuls():
        for gi in range(N_POOL_GROUPS):
            lo, hi = gi * POOL_GROUP, (gi + 1) * POOL_GROUP
            mixed = pool_buf[POOL_PAD:POOL_PAD + tile, lo:hi]
            pool_buf[POOL_PAD:POOL_PAD + tile, lo:hi] = (
                _dot(mixed, wpool_ref[gi]) * pscale_ref[:, lo:hi])

    def conv_block(part, bi):
        r0 = bi * blk
        for j in range(part * N_HEADS, (part + 1) * N_HEADS):
            lo, hi = j * HEAD_D, (j + 1) * HEAD_D
            ext = conv_buf[r0:r0 + CONV_PAD + blk, lo:hi]
            acc = ext[CONV_PAD:] * cw_ref[CONV_WIDTH - 1:CONV_WIDTH, lo:hi]
            for d in range(1, CONV_WIDTH):
                tap = CONV_WIDTH - 1 - d
                acc = acc + pltpu.roll(ext, d, 0)[CONV_PAD:] * cw_ref[tap:tap + 1, lo:hi]
            act = acc * _sigmoid(acc)
            if part < 2:
                scale = lax.rsqrt(jnp.sum(act * act, axis=-1, keepdims=True) + L2_EPS)
                if part == 0:
                    scale = scale * (HEAD_D ** -0.5)
                act = act * scale
            conv_buf[r0 + CONV_PAD:r0 + CONV_PAD + blk, lo:hi] = act

    pool_buf[POOL_PAD:POOL_PAD + tile, :] = jnp.dot(xb, win_ref[:, 0:D_MODEL],
                                                    preferred_element_type=_F32)
    for part in range(3):
        lo = OFF_QKV + part * D_MODEL
        conv_buf[CONV_PAD:CONV_PAD + tile, part * D_MODEL:(part + 1) * D_MODEL] = jnp.dot(
            xb, win_ref[:, lo:lo + D_MODEL], preferred_element_type=_F32)
    zy_ref[...] = jnp.dot(xb, win_ref[:, OFF_Z:OFF_GA], preferred_element_type=_F32)

    new_pool = pool_buf[tile + 1:tile + POOL_PAD, :]
    pool_out_ref[0] = new_pool
    for bi in reversed(range(n_blk)):
        pool_block(bi)
    pool_matmuls()
    pool_buf[1:POOL_PAD, :] = new_pool

    new_conv = conv_buf[CONV_PAD + tile - (CONV_WIDTH - 1):CONV_PAD + tile, :]
    conv_out_ref[0] = new_conv
    for bi in reversed(range(n_blk)):
        for part in range(3):
            conv_block(part, bi)
    conv_buf[CONV_PAD - (CONV_WIDTH - 1):CONV_PAD, :] = new_conv

    ri = lax.broadcasted_iota(jnp.int32, (chunk, chunk), 0)
    ci = lax.broadcasted_iota(jnp.int32, (chunk, chunk), 1)
    incl = ri >= ci
    strict = ri > ci

    heads = range(N_HEADS)
    q_lanes = [slice(h * HEAD_D, (h + 1) * HEAD_D) for h in heads]
    k_lanes = [slice(D_MODEL + h * HEAD_D, D_MODEL + (h + 1) * HEAD_D) for h in heads]
    v_lanes = [slice(2 * D_MODEL + h * HEAD_D, 2 * D_MODEL + (h + 1) * HEAD_D) for h in heads]
    n_chunks = tile // chunk
    units = [(c, h) for c in range(n_chunks) for h in heads]

    crow = lambda c: slice(CONV_PAD + c * chunk, CONV_PAD + (c + 1) * chunk)
    trow = lambda c: slice(c * chunk, (c + 1) * chunk)
    gc_cs = [gc[trow(c), :] for c in range(n_chunks)]
    gc_lasts = [g[chunk - 1:chunk, :] for g in gc_cs]
    e_alls = [jnp.exp(g) for g in gc_cs]
    tail_alls = [jnp.exp(gl - g) for gl, g in zip(gc_lasts, gc_cs)]
    carry_alls = [jnp.exp(gl) for gl in gc_lasts]
    col = lambda arr, h: arr[:, N_HEADS + h:N_HEADS + h + 1]
    kbs = {u: conv_buf[crow(u[0]), k_lanes[u[1]]] * beta[trow(u[0]), u[1]:u[1] + 1] for u in units}
    decays = {(c, h): jnp.where(
        incl, jnp.exp(col(gc_cs[c], h) - gc_t[N_HEADS + h:N_HEADS + h + 1, trow(c)]), 0.0)
        for c, h in units}
    qks = {(c, h): _dot_nt(jnp.concatenate([conv_buf[crow(c), q_lanes[h]], kbs[c, h]], axis=0),
                           conv_buf[crow(c), k_lanes[h]]) for c, h in units}
    attns = {u: (qks[u][0:chunk] * decays[u]).astype(_BF16) for u in units}
    t_invs = dict(zip(units, _unit_lower_inverses(
        [jnp.where(strict, qks[u][chunk:2 * chunk] * decays[u], 0.0).astype(_BF16) for u in units],
        chunk)))

    s_mats = [s_ref[h] for h in heads]
    for c in range(n_chunks):
        r0, r1 = c * chunk, (c + 1) * chunk
        e_cols = [col(e_alls[c], h) for h in heads]
        lhs = [jnp.concatenate([conv_buf[crow(c), q_lanes[h]] * e_cols[h], kbs[c, h] * e_cols[h]],
                               axis=0) for h in heads]
        k_tails = [conv_buf[crow(c), k_lanes[h]] * col(tail_alls[c], h) for h in heads]
        qss = [_dot(lhs[h], s_mats[h]) for h in heads]
        v_news = [_dot(t_invs[c, h], conv_buf[crow(c), v_lanes[h]] * beta[r0:r1, h:h + 1]
                       - qss[h][chunk:2 * chunk]) for h in heads]
        s_mats = [s_mats[h] * col(carry_alls[c], h) + _dot_tn(k_tails[h], v_news[h]) for h in heads]
        for h in heads:
            o = qss[h][0:chunk] + _dot(attns[c, h], v_news[h])
            o = o * lax.rsqrt(jnp.mean(o * o, axis=-1, keepdims=True) + RMS_EPS) * og_ref[...]
            zc = zy_ref[r0:r1, q_lanes[h]]
            zy_ref[r0:r1, q_lanes[h]] = o * (zc * _sigmoid(zc))
    for h in heads:
        s_ref[h] = s_mats[h]

    ga = jnp.dot(xb, win_ref[:, OFF_GA:OFF_GB], preferred_element_type=_F32)
    merged = _sigmoid(ga) * pool_buf[POOL_PAD:POOL_PAD + tile, :]
    gb = jnp.dot(xb, win_ref[:, OFF_GB:OFF_BETA], preferred_element_type=_F32)
    merged = merged + _sigmoid(gb) * zy_ref[...]
    y = ALPHA * res_ref[0] + _dot(merged, wout_ref[...])
    y_ref[0] = _layer_norm(y, l1g_ref[...], l1b_ref[...])

    @pl.when(t == n_t - 1)
    def _():
        s_out_ref[0] = s_ref[...]


def _ffn_kernel(x_ref, w1_ref, b1_ref, w2_ref, b2_ref, g_ref, b_ref, y_ref):
    x = x_ref[...]
    xb = x.astype(_BF16)
    acc = ALPHA * x + b2_ref[...]
    for j in range(D_FF // D_MODEL):
        lo, hi = j * D_MODEL, (j + 1) * D_MODEL
        h = jnp.dot(xb, w1_ref[:, lo:hi], preferred_element_type=_F32) + b1_ref[:, lo:hi]
        h = jnp.square(jnp.maximum(h, 0.0))
        acc = acc + jnp.dot(h.astype(_BF16), w2_ref[lo:hi, :], preferred_element_type=_F32)
    y_ref[...] = _layer_norm(acc, g_ref[...], b_ref[...])


def _const_spec(shape):
    return pl.BlockSpec(shape, lambda *_: (0,) * len(shape), pipeline_mode=pl.Buffered(1))


def _layer_spec(shape, layer):
    return pl.BlockSpec((None,) + shape, lambda *_: (layer,) + (0,) * len(shape),
                        pipeline_mode=pl.Buffered(1))


def _mixer_call(x, pool_hist, conv_hist, s0, lw, *, layer, chunk, pos0):
    apply_ln_in = layer == 0
    nb, seq, _ = x.shape
    tile = min(seq, MIXER_TILE)
    n_t = seq // tile
    kern = functools.partial(_mixer_kernel, tile=tile, chunk=chunk, pos0=pos0,
                             apply_ln_in=apply_ln_in)
    per_stream = lambda shape: pl.BlockSpec((1,) + shape, lambda b, t: (b,) + (0,) * len(shape))
    in_specs = [
        pl.BlockSpec((1, tile, D_MODEL), lambda b, t: (b, t, 0)),
        per_stream((POOL_PAD, D_MODEL)),
        per_stream((CONV_PAD, CONV_CH)),
        per_stream((N_HEADS, HEAD_D, HEAD_D)),
        _const_spec((1, D_MODEL)), _const_spec((1, D_MODEL)),
        _layer_spec((D_MODEL, MAIN_WIDTH), layer),
        _layer_spec((D_MODEL, V7X_LANES), layer),
        _layer_spec((CONV_WIDTH, CONV_CH), layer),
        _layer_spec((1, V7X_LANES), layer), _layer_spec((1, V7X_LANES), layer),
        _layer_spec((1, HEAD_D), layer),
        _layer_spec((N_POOL_GROUPS, POOL_GROUP, POOL_GROUP), layer),
        _layer_spec((1, D_MODEL), layer),
        _layer_spec((D_MODEL, D_MODEL), layer),
        _layer_spec((1, D_MODEL), layer), _layer_spec((1, D_MODEL), layer),
    ]
    out_shape = (
        jax.ShapeDtypeStruct((nb, seq, D_MODEL), _F32),
        jax.ShapeDtypeStruct((nb, POOL_STATE, D_MODEL), _F32),
        jax.ShapeDtypeStruct((nb, CONV_WIDTH - 1, CONV_CH), _F32),
        jax.ShapeDtypeStruct((nb, N_HEADS, HEAD_D, HEAD_D), _F32),
    )
    out_specs = (
        pl.BlockSpec((1, tile, D_MODEL), lambda b, t: (b, t, 0)),
        per_stream((POOL_STATE, D_MODEL)),
        per_stream((CONV_WIDTH - 1, CONV_CH)),
        per_stream((N_HEADS, HEAD_D, HEAD_D)),
    )
    scratch = [
        pltpu.VMEM((POOL_PAD + tile, D_MODEL), _F32),
        pltpu.VMEM((CONV_PAD + tile, CONV_CH), _F32),
        pltpu.VMEM((N_HEADS, HEAD_D, HEAD_D), _F32),
        pltpu.VMEM((tile, D_MODEL), _F32),
    ]
    return pl.pallas_call(
        kern,
        grid=(nb, n_t),
        in_specs=in_specs,
        out_specs=out_specs,
        out_shape=out_shape,
        scratch_shapes=scratch,
        compiler_params=pltpu.CompilerParams(
            dimension_semantics=("arbitrary", "arbitrary"),
            vmem_limit_bytes=V7X_VMEM_LIMIT_BYTES),
        name="mixer",
    )(x, pool_hist, conv_hist, s0, *lw)


def _ffn_call(x, w1, b1, w2, b2, g, b, *, layer):
    n = x.shape[0]
    row_spec = pl.BlockSpec((FFN_TILE, D_MODEL), lambda i: (i, 0))
    return pl.pallas_call(
        _ffn_kernel,
        grid=(n // FFN_TILE,),
        in_specs=[row_spec,
                  _layer_spec((D_MODEL, D_FF), layer), _layer_spec((1, D_FF), layer),
                  _layer_spec((D_FF, D_MODEL), layer), _layer_spec((1, D_MODEL), layer),
                  _layer_spec((1, D_MODEL), layer), _layer_spec((1, D_MODEL), layer)],
        out_specs=row_spec,
        out_shape=jax.ShapeDtypeStruct((n, D_MODEL), _F32),
        compiler_params=pltpu.CompilerParams(
            dimension_semantics=("arbitrary",),
            vmem_limit_bytes=V7X_VMEM_LIMIT_BYTES),
        name="ffn",
    )(x, w1, b1, w2, b2, g, b)


def kernel(x_prompt, x_sample, state_pool, state_conv, state_delta, ln_in_g, ln_in_b, w_in, conv_w, a_log,
           dt_bias, o_gain, w_pool, pool_scale, w_out, ln1_g, ln1_b, w_ff1, b_ff1, w_ff2, b_ff2, ln2_g, ln2_b):
    nbp, seq_p, _ = x_prompt.shape
    nbs, seq_s, _ = x_sample.shape
    row = lambda v: v.reshape(1, -1).astype(_F32)
    rows = lambda v: v.reshape(DEPTH, 1, -1).astype(_F32)
    gate_lanes = lambda v: jnp.pad(v.astype(_F32), ((0, 0), (N_HEADS, V7X_LANES - 2 * N_HEADS))
                                   ).reshape(DEPTH, 1, V7X_LANES)
    w_ba = jnp.pad(w_in[:, :, OFF_BETA:], ((0, 0), (0, 0), (0, V7X_LANES - 2 * N_HEADS)))
    lw = (row(ln_in_g), row(ln_in_b), w_in.astype(_BF16), w_ba.astype(_BF16), conv_w.astype(_F32),
          gate_lanes(a_log), gate_lanes(dt_bias), rows(o_gain), w_pool.astype(_BF16),
          rows(pool_scale), w_out.astype(_BF16), rows(ln1_g), rows(ln1_b))
    fw = (w_ff1.astype(_BF16), rows(b_ff1), w_ff2.astype(_BF16), rows(b_ff2), rows(ln2_g), rows(ln2_b))

    xp, xs = x_prompt, x_sample
    outs = {k: [] for k in ("pp", "cp", "sp", "ps", "cs", "ss")}
    zeros_pool = jnp.zeros((nbp, POOL_PAD, D_MODEL), _F32)
    zeros_conv = jnp.zeros((nbp, CONV_PAD, CONV_CH), _F32)
    zeros_s = jnp.zeros((nbp, N_HEADS, HEAD_D, HEAD_D), _F32)
    pool_hist = jnp.pad(state_pool, ((0, 0), (0, 0), (POOL_PAD - POOL_STATE, 0), (0, 0)))
    conv_hist = jnp.pad(state_conv, ((0, 0), (0, 0), (CONV_PAD - (CONV_WIDTH - 1), 0), (0, 0)))
    for l in range(DEPTH):
        xp, pp, cp, sp = _mixer_call(xp, zeros_pool, zeros_conv, zeros_s, lw,
                                     layer=l, chunk=PROMPT_CHUNK, pos0=0)
        xp = _ffn_call(xp.reshape(nbp * seq_p, D_MODEL), *fw, layer=l).reshape(nbp, seq_p, D_MODEL)

        xs, ps, cs, ss = _mixer_call(xs, pool_hist[l], conv_hist[l], state_delta[l], lw,
                                     layer=l, chunk=min(CHUNK, seq_s), pos0=PAST_LEN)
        xs = _ffn_call(xs.reshape(nbs * seq_s, D_MODEL), *fw, layer=l).reshape(nbs, seq_s, D_MODEL)

        for k, v in zip(("pp", "cp", "sp", "ps", "cs", "ss"), (pp, cp, sp, ps, cs, ss)):
            outs[k].append(v)
    return (xp, xs, jnp.stack(outs["pp"]), jnp.stack(outs["cp"]), jnp.stack(outs["sp"]),
            jnp.stack(outs["ps"]), jnp.stack(outs["cs"]), jnp.stack(outs["ss"]))
```

```python
import functools
import math

import jax
import jax.numpy as jnp
from jax import lax
from jax.experimental import pallas as pl
from jax.experimental.pallas import tpu as pltpu

D_MODEL = 1024
DEPTH = 2
PAST_LEN = 1024
CHUNK = 64
PROMPT_CHUNK = 128
POOL_WINDOWS = (2, 4, 8, 16)
N_POOL_GROUPS = 4
POOL_GROUP = D_MODEL // N_POOL_GROUPS
POOL_STATE = 15
HEAD_D = 128
N_HEADS = D_MODEL // HEAD_D
CONV_WIDTH = 4
CONV_CH = 3 * D_MODEL
D_FF = 4 * D_MODEL
ALPHA = (2 * DEPTH) ** 0.25
LN_EPS = 1e-5
RMS_EPS = 1e-6
L2_EPS = 1e-6
OFF_QKV = D_MODEL
OFF_Z = OFF_QKV + CONV_CH
OFF_GA = OFF_Z + D_MODEL
OFF_GB = OFF_GA + D_MODEL
OFF_BETA = OFF_GB + D_MODEL
MAIN_WIDTH = OFF_BETA
NEG_LOG2_E = -1.0 / math.log(2.0)

V7X_LANES = 128
V7X_SUBLANES = 8
BF16_ROWS_PER_TILE = 2 * V7X_SUBLANES
V7X_VMEM_LIMIT_BYTES = 60 * 1024 * 1024
WIDE_BLOCK = BF16_ROWS_PER_TILE

POOL_PAD = 16
CONV_PAD = 8
ROW_BLOCK = 128
MIXER_TILE = 512
FFN_TILE = 1024

_BF16 = jnp.bfloat16
_F32 = jnp.float32


def _dot(a, b):
    return jnp.dot(a.astype(_BF16), b.astype(_BF16), preferred_element_type=_F32)


def _dot_nt(a, b):
    return lax.dot_general(a.astype(_BF16), b.astype(_BF16), (((1,), (1,)), ((), ())),
                           preferred_element_type=_F32)


def _dot_tn(a, b):
    return lax.dot_general(a.astype(_BF16), b.astype(_BF16), (((0,), (0,)), ((), ())),
                           preferred_element_type=_F32)


def _sigmoid(x):
    return 1.0 / (1.0 + jnp.exp2(x * NEG_LOG2_E))


def _softplus(x):
    return jnp.maximum(x, 0.0) + jnp.log1p(jnp.exp(-jnp.abs(x)))


def _layer_norm(x, g, b):
    mu = jnp.mean(x, axis=-1, keepdims=True)
    xc = x - mu
    var = jnp.mean(xc * xc, axis=-1, keepdims=True)
    return xc * lax.rsqrt(var + LN_EPS) * g + b


def _unit_lower_inverses(mats, n):
    as_mask = lambda cond: jnp.where(cond, 1.0, 0.0).astype(_BF16)
    mm = lambda a, b: jnp.dot(a, b, preferred_element_type=_F32)
    ri = lax.broadcasted_iota(jnp.int32, (n, n), 0)
    ci = lax.broadcasted_iota(jnp.int32, (n, n), 1)
    same = lambda s: (ri >> s) == (ci >> s)
    s = 1
    if n == V7X_LANES:
        wb = WIDE_BLOCK
        wi = lax.broadcasted_iota(jnp.int32, (wb, n), 0)
        wj = lax.broadcasted_iota(jnp.int32, (wb, n), 1) & (wb - 1)
        wsame = lambda lvl: (wi >> lvl) == (wj >> lvl)
        diag_blocks = as_mask(same(int(math.log2(wb))))

        def expand(w):
            return jnp.concatenate([w] * (n // wb), axis=0) * diag_blocks

        def gather(a):
            masked = a * diag_blocks
            out = masked[0:wb]
            for b in range(1, n // wb):
                out = out + masked[b * wb:(b + 1) * wb]
            return out

        wides = [gather(a) for a in mats]
        winvs = [as_mask(wi == wj) - w * as_mask(wsame(1)) for w in wides]
        while (1 << s) < wb:
            off = as_mask(wsame(s + 1) & jnp.logical_not(wsame(s)))
            part = [mm(winv, expand(w * off)) for winv, w in zip(winvs, wides)]
            winvs = [winv - mm(p.astype(_BF16), expand(winv)).astype(_BF16)
                     for winv, p in zip(winvs, part)]
            s += 1
        invs = [expand(winv) for winv in winvs]
    else:
        invs = [as_mask(ri == ci) - a * as_mask(same(1)) for a in mats]
    while (1 << s) < n:
        off = as_mask(same(s + 1) & jnp.logical_not(same(s)))
        part = [mm(inv, a * off) for inv, a in zip(invs, mats)]
        invs = [inv - mm(p.astype(_BF16), inv).astype(_BF16) for inv, p in zip(invs, part)]
        s += 1
    return invs


def _mixer_kernel(x_ref, ph_ref, ch_ref, s0_ref, lng_ref, lnb_ref, win_ref, wba_ref, cw_ref,
                  alog_ref, dtb_ref, og_ref, wpool_ref, pscale_ref, wout_ref, l1g_ref, l1b_ref,
                  y_ref, pool_out_ref, conv_out_ref, s_out_ref,
                  pool_buf, conv_buf, s_ref, zy_ref,
                  *, tile, chunk, pos0, apply_ln_in):
    t = pl.program_id(1)
    n_t = pl.num_programs(1)
    blk = min(tile, ROW_BLOCK)
    n_blk = tile // blk

    @pl.when(t == 0)
    def _():
        pool_buf[0:POOL_PAD, :] = ph_ref[0]
        conv_buf[0:CONV_PAD, :] = ch_ref[0]
        s_ref[...] = s0_ref[0]

    x = x_ref[0]
    if apply_ln_in:
        x = _layer_norm(x, lng_ref[...], lnb_ref[...])
        y_ref[0] = x
    res_ref = y_ref if apply_ln_in else x_ref
    xb = x.astype(_BF16)

    raw = jnp.dot(xb, wba_ref[...], preferred_element_type=_F32)
    beta = _sigmoid(raw)
    g = -jnp.exp(alog_ref[...]) * _softplus(raw + dtb_ref[...])
    rowc = lax.broadcasted_iota(jnp.int32, (tile, V7X_LANES), 0) & (chunk - 1)
    gc = g
    sh = 1
    while sh < chunk:
        gc = gc + jnp.where(rowc >= sh, pltpu.roll(gc, sh, 0), 0.0)
        sh *= 2
    gc_t = gc.T

    row = lax.broadcasted_iota(jnp.int32, (blk, V7X_LANES), 0)

    def pool_block(bi):
        r0 = bi * blk
        pos1 = row + (t * tile + (pos0 + 1 + r0))
        for gi, w in enumerate(POOL_WINDOWS):
            cnt = jnp.minimum(pos1, w).astype(_F32)
            for lo in range(gi * POOL_GROUP, (gi + 1) * POOL_GROUP, V7X_LANES):
                ext = pool_buf[r0:r0 + POOL_PAD + blk, lo:lo + V7X_LANES]
                acc = ext
                sh = 1
                while sh < w:
                    acc = acc + pltpu.roll(acc, sh, 0)
                    sh *= 2
                pool_buf[r0 + POOL_PAD:r0 + POOL_PAD + blk, lo:lo + V7X_LANES] = (
                    acc[POOL_PAD:] / cnt - ext[POOL_PAD:])

    def pool_matmuls():
        for gi in range(N_POOL_GROUPS):
            lo, hi = gi * POOL_GROUP, (gi + 1) * POOL_GROUP
            mixed = pool_buf[POOL_PAD:POOL_PAD + tile, lo:hi]
            pool_buf[POOL_PAD:POOL_PAD + tile, lo:hi] = (
                _dot(mixed, wpool_ref[gi]) * pscale_ref[:, lo:hi])

    def conv_block(part, bi):
        r0 = bi * blk
        for j in range(part * N_HEADS, (part + 1) * N_HEADS):
            lo, hi = j * HEAD_D, (j + 1) * HEAD_D
            ext = conv_buf[r0:r0 + CONV_PAD + blk, lo:hi]
            acc = ext[CONV_PAD:] * cw_ref[CONV_WIDTH - 1:CONV_WIDTH, lo:hi]
            for d in range(1, CONV_WIDTH):
                tap = CONV_WIDTH - 1 - d
                acc = acc + pltpu.roll(ext, d, 0)[CONV_PAD:] * cw_ref[tap:tap + 1, lo:hi]
            act = acc * _sigmoid(acc)
            if part < 2:
                scale = lax.rsqrt(jnp.sum(act * act, axis=-1, keepdims=True) + L2_EPS)
                if part == 0:
                    scale = scale * (HEAD_D ** -0.5)
                act = act * scale
            conv_buf[r0 + CONV_PAD:r0 + CONV_PAD + blk, lo:hi] = act

    pool_buf[POOL_PAD:POOL_PAD + tile, :] = jnp.dot(xb, win_ref[:, 0:D_MODEL],
                                                    preferred_element_type=_F32)
    for part in range(3):
        lo = OFF_QKV + part * D_MODEL
        conv_buf[CONV_PAD:CONV_PAD + tile, part * D_MODEL:(part + 1) * D_MODEL] = jnp.dot(
            xb, win_ref[:, lo:lo + D_MODEL], preferred_element_type=_F32)
    zy_ref[...] = jnp.dot(xb, win_ref[:, OFF_Z:OFF_GA], preferred_element_type=_F32)

    new_pool = pool_buf[tile + 1:tile + POOL_PAD, :]
    pool_out_ref[0] = new_pool
    for bi in reversed(range(n_blk)):
        pool_block(bi)
    pool_matmuls()
    pool_buf[1:POOL_PAD, :] = new_pool

    new_conv = conv_buf[CONV_PAD + tile - (CONV_WIDTH - 1):CONV_PAD + tile, :]
    conv_out_ref[0] = new_conv
    for bi in reversed(range(n_blk)):
        for part in range(3):
            conv_block(part, bi)
    conv_buf[CONV_PAD - (CONV_WIDTH - 1):CONV_PAD, :] = new_conv

    ri = lax.broadcasted_iota(jnp.int32, (chunk, chunk), 0)
    ci = lax.broadcasted_iota(jnp.int32, (chunk, chunk), 1)
    incl = ri >= ci
    strict = ri > ci

    heads = range(N_HEADS)
    q_lanes = [slice(h * HEAD_D, (h + 1) * HEAD_D) for h in heads]
    k_lanes = [slice(D_MODEL + h * HEAD_D, D_MODEL + (h + 1) * HEAD_D) for h in heads]
    v_lanes = [slice(2 * D_MODEL + h * HEAD_D, 2 * D_MODEL + (h + 1) * HEAD_D) for h in heads]
    n_chunks = tile // chunk
    units = [(c, h) for c in range(n_chunks) for h in heads]

    crow = lambda c: slice(CONV_PAD + c * chunk, CONV_PAD + (c + 1) * chunk)
    trow = lambda c: slice(c * chunk, (c + 1) * chunk)
    gc_cs = [gc[trow(c), :] for c in range(n_chunks)]
    gc_lasts = [g[chunk - 1:chunk, :] for g in gc_cs]
    e_alls = [jnp.exp(g) for g in gc_cs]
    tail_alls = [jnp.exp(gl - g) for gl, g in zip(gc_lasts, gc_cs)]
    carry_alls = [jnp.exp(gl) for gl in gc_lasts]
    col = lambda arr, h: arr[:, N_HEADS + h:N_HEADS + h + 1]
    kbs = {u: conv_buf[crow(u[0]), k_lanes[u[1]]] * beta[trow(u[0]), u[1]:u[1] + 1] for u in units}
    decays = {(c, h): jnp.where(
        incl, jnp.exp(col(gc_cs[c], h) - gc_t[N_HEADS + h:N_HEADS + h + 1, trow(c)]), 0.0)
        for c, h in units}
    qks = {(c, h): _dot_nt(jnp.concatenate([conv_buf[crow(c), q_lanes[h]], kbs[c, h]], axis=0),
                           conv_buf[crow(c), k_lanes[h]]) for c, h in units}
    attns = {u: (qks[u][0:chunk] * decays[u]).astype(_BF16) for u in units}
    t_invs = dict(zip(units, _unit_lower_inverses(
        [jnp.where(strict, qks[u][chunk:2 * chunk] * decays[u], 0.0).astype(_BF16) for u in units],
        chunk)))

    s_mats = [s_ref[h] for h in heads]
    for c in range(n_chunks):
        r0, r1 = c * chunk, (c + 1) * chunk
        e_cols = [col(e_alls[c], h) for h in heads]
        lhs = [jnp.concatenate([conv_buf[crow(c), q_lanes[h]] * e_cols[h], kbs[c, h] * e_cols[h]],
                               axis=0) for h in heads]
        k_tails = [conv_buf[crow(c), k_lanes[h]] * col(tail_alls[c], h) for h in heads]
        qss = [_dot(lhs[h], s_mats[h]) for h in heads]
        v_news = [_dot(t_invs[c, h], conv_buf[crow(c), v_lanes[h]] * beta[r0:r1, h:h + 1]
                       - qss[h][chunk:2 * chunk]) for h in heads]
        s_mats = [s_mats[h] * col(carry_alls[c], h) + _dot_tn(k_tails[h], v_news[h]) for h in heads]
        for h in heads:
            o = qss[h][0:chunk] + _dot(attns[c, h], v_news[h])
            o = o * lax.rsqrt(jnp.mean(o * o, axis=-1, keepdims=True) + RMS_EPS) * og_ref[...]
            zc = zy_ref[r0:r1, q_lanes[h]]
            zy_ref[r0:r1, q_lanes[h]] = o * (zc * _sigmoid(zc))
    for h in heads:
        s_ref[h] = s_mats[h]

    ga = jnp.dot(xb, win_ref[:, OFF_GA:OFF_GB], preferred_element_type=_F32)
    merged = _sigmoid(ga) * pool_buf[POOL_PAD:POOL_PAD + tile, :]
    gb = jnp.dot(xb, win_ref[:, OFF_GB:OFF_BETA], preferred_element_type=_F32)
    merged = merged + _sigmoid(gb) * zy_ref[...]
    y = ALPHA * res_ref[0] + _dot(merged, wout_ref[...])
    y_ref[0] = _layer_norm(y, l1g_ref[...], l1b_ref[...])

    @pl.when(t == n_t - 1)
    def _():
        s_out_ref[0] = s_ref[...]


def _ffn_kernel(x_ref, w1_ref, b1_ref, w2_ref, b2_ref, g_ref, b_ref, y_ref):
    x = x_ref[...]
    xb = x.astype(_BF16)
    acc = ALPHA * x + b2_ref[...]
    for j in range(D_FF // D_MODEL):
        lo, hi = j * D_MODEL, (j + 1) * D_MODEL
        h = jnp.dot(xb, w1_ref[:, lo:hi], preferred_element_type=_F32) + b1_ref[:, lo:hi]
        h = jnp.square(jnp.maximum(h, 0.0))
        acc = acc + jnp.dot(h.astype(_BF16), w2_ref[lo:hi, :], preferred_element_type=_F32)
    y_ref[...] = _layer_norm(acc, g_ref[...], b_ref[...])


def _const_spec(shape):
    return pl.BlockSpec(shape, lambda *_: (0,) * len(shape), pipeline_mode=pl.Buffered(1))


def _layer_spec(shape, layer):
    return pl.BlockSpec((None,) + shape, lambda *_: (layer,) + (0,) * len(shape),
                        pipeline_mode=pl.Buffered(1))


def _mixer_call(x, pool_hist, conv_hist, s0, lw, *, layer, chunk, pos0):
    apply_ln_in = layer == 0
    nb, seq, _ = x.shape
    tile = min(seq, MIXER_TILE)
    n_t = seq // tile
    kern = functools.partial(_mixer_kernel, tile=tile, chunk=chunk, pos0=pos0,
                             apply_ln_in=apply_ln_in)
    per_stream = lambda shape: pl.BlockSpec((1,) + shape, lambda b, t: (b,) + (0,) * len(shape))
    in_specs = [
        pl.BlockSpec((1, tile, D_MODEL), lambda b, t: (b, t, 0)),
        per_stream((POOL_PAD, D_MODEL)),
        per_stream((CONV_PAD, CONV_CH)),
        per_stream((N_HEADS, HEAD_D, HEAD_D)),
        _const_spec((1, D_MODEL)), _const_spec((1, D_MODEL)),
        _layer_spec((D_MODEL, MAIN_WIDTH), layer),
        _layer_spec((D_MODEL, V7X_LANES), layer),
        _layer_spec((CONV_WIDTH, CONV_CH), layer),
        _layer_spec((1, V7X_LANES), layer), _layer_spec((1, V7X_LANES), layer),
        _layer_spec((1, HEAD_D), layer),
        _layer_spec((N_POOL_GROUPS, POOL_GROUP, POOL_GROUP), layer),
        _layer_spec((1, D_MODEL), layer),
        _layer_spec((D_MODEL, D_MODEL), layer),
        _layer_spec((1, D_MODEL), layer), _layer_spec((1, D_MODEL), layer),
    ]
    out_shape = (
        jax.ShapeDtypeStruct((nb, seq, D_MODEL), _F32),
        jax.ShapeDtypeStruct((nb, POOL_STATE, D_MODEL), _F32),
        jax.ShapeDtypeStruct((nb, CONV_WIDTH - 1, CONV_CH), _F32),
        jax.ShapeDtypeStruct((nb, N_HEADS, HEAD_D, HEAD_D), _F32),
    )
    out_specs = (
        pl.BlockSpec((1, tile, D_MODEL), lambda b, t: (b, t, 0)),
        per_stream((POOL_STATE, D_MODEL)),
        per_stream((CONV_WIDTH - 1, CONV_CH)),
        per_stream((N_HEADS, HEAD_D, HEAD_D)),
    )
    scratch = [
        pltpu.VMEM((POOL_PAD + tile, D_MODEL), _F32),
        pltpu.VMEM((CONV_PAD + tile, CONV_CH), _F32),
        pltpu.VMEM((N_HEADS, HEAD_D, HEAD_D), _F32),
        pltpu.VMEM((tile, D_MODEL), _F32),
    ]
    return pl.pallas_call(
        kern,
        grid=(nb, n_t),
        in_specs=in_specs,
        out_specs=out_specs,
        out_shape=out_shape,
        scratch_shapes=scratch,
        compiler_params=pltpu.CompilerParams(
            dimension_semantics=("arbitrary", "arbitrary"),
            vmem_limit_bytes=V7X_VMEM_LIMIT_BYTES),
        name="mixer",
    )(x, pool_hist, conv_hist, s0, *lw)


def _ffn_call(x, w1, b1, w2, b2, g, b, *, layer):
    n = x.shape[0]
    tile = min(n, FFN_TILE)
    row_spec = pl.BlockSpec((tile, D_MODEL), lambda i: (i, 0))
    return pl.pallas_call(
        _ffn_kernel,
        grid=(n // tile,),
        in_specs=[row_spec,
                  _layer_spec((D_MODEL, D_FF), layer), _layer_spec((1, D_FF), layer),
                  _layer_spec((D_FF, D_MODEL), layer), _layer_spec((1, D_MODEL), layer),
                  _layer_spec((1, D_MODEL), layer), _layer_spec((1, D_MODEL), layer)],
        out_specs=row_spec,
        out_shape=jax.ShapeDtypeStruct((n, D_MODEL), _F32),
        compiler_params=pltpu.CompilerParams(
            dimension_semantics=("arbitrary",),
            vmem_limit_bytes=V7X_VMEM_LIMIT_BYTES),
        name="ffn",
    )(x, w1, b1, w2, b2, g, b)


def kernel(x_prompt, x_sample, state_pool, state_conv, state_delta, ln_in_g, ln_in_b, w_in, conv_w, a_log,
           dt_bias, o_gain, w_pool, pool_scale, w_out, ln1_g, ln1_b, w_ff1, b_ff1, w_ff2, b_ff2, ln2_g, ln2_b):
    nbp, seq_p, _ = x_prompt.shape
    nbs, seq_s, _ = x_sample.shape
    row = lambda v: v.reshape(1, -1).astype(_F32)
    rows = lambda v: v.reshape(DEPTH, 1, -1).astype(_F32)
    gate_lanes = lambda v: jnp.pad(v.astype(_F32), ((0, 0), (N_HEADS, V7X_LANES - 2 * N_HEADS))
                                   ).reshape(DEPTH, 1, V7X_LANES)
    w_ba = jnp.pad(w_in[:, :, OFF_BETA:], ((0, 0), (0, 0), (0, V7X_LANES - 2 * N_HEADS)))
    lw = (row(ln_in_g), row(ln_in_b), w_in.astype(_BF16), w_ba.astype(_BF16), conv_w.astype(_F32),
          gate_lanes(a_log), gate_lanes(dt_bias), rows(o_gain), w_pool.astype(_BF16),
          rows(pool_scale), w_out.astype(_BF16), rows(ln1_g), rows(ln1_b))
    fw = (w_ff1.astype(_BF16), rows(b_ff1), w_ff2.astype(_BF16), rows(b_ff2), rows(ln2_g), rows(ln2_b))

    xp, xs = x_prompt, x_sample
    outs = {k: [] for k in ("pp", "cp", "sp", "ps", "cs", "ss")}
    zeros_pool = jnp.zeros((nbp, POOL_PAD, D_MODEL), _F32)
    zeros_conv = jnp.zeros((nbp, CONV_PAD, CONV_CH), _F32)
    zeros_s = jnp.zeros((nbp, N_HEADS, HEAD_D, HEAD_D), _F32)
    pool_hist = jnp.pad(state_pool, ((0, 0), (0, 0), (POOL_PAD - POOL_STATE, 0), (0, 0)))
    conv_hist = jnp.pad(state_conv, ((0, 0), (0, 0), (CONV_PAD - (CONV_WIDTH - 1), 0), (0, 0)))
    for l in range(DEPTH):
        xp, pp, cp, sp = _mixer_call(xp, zeros_pool, zeros_conv, zeros_s, lw,
                                     layer=l, chunk=PROMPT_CHUNK, pos0=0)
        xp = _ffn_call(xp.reshape(nbp * seq_p, D_MODEL), *fw, layer=l).reshape(nbp, seq_p, D_MODEL)

        xs, ps, cs, ss = _mixer_call(xs, pool_hist[l], conv_hist[l], state_delta[l], lw,
                                     layer=l, chunk=min(CHUNK, seq_s), pos0=PAST_LEN)
        xs = _ffn_call(xs.reshape(nbs * seq_s, D_MODEL), *fw, layer=l).reshape(nbs, seq_s, D_MODEL)

        for k, v in zip(("pp", "cp", "sp", "ps", "cs", "ss"), (pp, cp, sp, ps, cs, ss)):
            outs[k].append(v)
    return (xp, xs, jnp.stack(outs["pp"]), jnp.stack(outs["cp"]), jnp.stack(outs["sp"]),
            jnp.stack(outs["ps"]), jnp.stack(outs["cs"]), jnp.stack(outs["ss"]))
```

```python
import functools
import math

import jax
import jax.numpy as jnp
from jax import lax
from jax.experimental import pallas as pl
from jax.experimental.pallas import tpu as pltpu

D_MODEL = 1024
DEPTH = 2
PAST_LEN = 1024
CHUNK = 64
PROMPT_CHUNK = 128
POOL_WINDOWS = (2, 4, 8, 16)
N_POOL_GROUPS = 4
POOL_GROUP = D_MODEL // N_POOL_GROUPS
POOL_STATE = 15
HEAD_D = 128
N_HEADS = D_MODEL // HEAD_D
CONV_WIDTH = 4
CONV_CH = 3 * D_MODEL
D_FF = 4 * D_MODEL
ALPHA = (2 * DEPTH) ** 0.25
LN_EPS = 1e-5
RMS_EPS = 1e-6
L2_EPS = 1e-6
OFF_QKV = D_MODEL
OFF_Z = OFF_QKV + CONV_CH
OFF_GA = OFF_Z + D_MODEL
OFF_GB = OFF_GA + D_MODEL
OFF_BETA = OFF_GB + D_MODEL
MAIN_WIDTH = OFF_BETA
NEG_LOG2_E = -1.0 / math.log(2.0)

V7X_LANES = 128
V7X_SUBLANES = 8
BF16_ROWS_PER_TILE = 2 * V7X_SUBLANES
V7X_VMEM_LIMIT_BYTES = 60 * 1024 * 1024
WIDE_BLOCK = BF16_ROWS_PER_TILE

POOL_PAD = 16
CONV_PAD = 8
ROW_BLOCK = 128
MIXER_TILE = 512
PRM_CONV, PRM_LN_IN_G, PRM_LN_IN_B, PRM_ALOG, PRM_DT, PRM_OGAIN, PRM_PSCALE, PRM_LN1_G, PRM_LN1_B = (
    0, 4, 5, 6, 7, 8, 9, 10, 11)
PRM_ROWS = 16
FPRM_B1, FPRM_B2, FPRM_LN2_G, FPRM_LN2_B = 0, 1, 2, 3
FPRM_ROWS = 8
FFN_TILE = 1024

_BF16 = jnp.bfloat16
_F32 = jnp.float32


def _dot(a, b):
    return jnp.dot(a.astype(_BF16), b.astype(_BF16), preferred_element_type=_F32)


def _dot_nt(a, b):
    return lax.dot_general(a.astype(_BF16), b.astype(_BF16), (((1,), (1,)), ((), ())),
                           preferred_element_type=_F32)


def _dot_tn(a, b):
    return lax.dot_general(a.astype(_BF16), b.astype(_BF16), (((0,), (0,)), ((), ())),
                           preferred_element_type=_F32)


def _sigmoid(x):
    return 1.0 / (1.0 + jnp.exp2(x * NEG_LOG2_E))


def _softplus(x):
    return jnp.maximum(x, 0.0) + jnp.log1p(jnp.exp(-jnp.abs(x)))


def _layer_norm(x, g, b):
    mu = jnp.mean(x, axis=-1, keepdims=True)
    xc = x - mu
    var = jnp.mean(xc * xc, axis=-1, keepdims=True)
    return xc * lax.rsqrt(var + LN_EPS) * g + b


def _unit_lower_inverses(mats, n):
    as_mask = lambda cond: jnp.where(cond, 1.0, 0.0).astype(_BF16)
    mm = lambda a, b: jnp.dot(a, b, preferred_element_type=_F32)
    ri = lax.broadcasted_iota(jnp.int32, (n, n), 0)
    ci = lax.broadcasted_iota(jnp.int32, (n, n), 1)
    same = lambda s: (ri >> s) == (ci >> s)
    s = 1
    if n == V7X_LANES:
        wb = WIDE_BLOCK
        wi = lax.broadcasted_iota(jnp.int32, (wb, n), 0)
        wj = lax.broadcasted_iota(jnp.int32, (wb, n), 1) & (wb - 1)
        wsame = lambda lvl: (wi >> lvl) == (wj >> lvl)
        diag_blocks = as_mask(same(int(math.log2(wb))))

        def expand(w):
            return jnp.concatenate([w] * (n // wb), axis=0) * diag_blocks

        def gather(a):
            masked = a * diag_blocks
            out = masked[0:wb]
            for b in range(1, n // wb):
                out = out + masked[b * wb:(b + 1) * wb]
            return out

        wides = [gather(a) for a in mats]
        winvs = [as_mask(wi == wj) - w * as_mask(wsame(1)) for w in wides]
        while (1 << s) < wb:
            off = as_mask(wsame(s + 1) & jnp.logical_not(wsame(s)))
            part = [mm(winv, expand(w * off)) for winv, w in zip(winvs, wides)]
            winvs = [winv - mm(p.astype(_BF16), expand(winv)).astype(_BF16)
                     for winv, p in zip(winvs, part)]
            s += 1
        invs = [expand(winv) for winv in winvs]
    else:
        invs = [as_mask(ri == ci) - a * as_mask(same(1)) for a in mats]
    while (1 << s) < n:
        off = as_mask(same(s + 1) & jnp.logical_not(same(s)))
        part = [mm(inv, a * off) for inv, a in zip(invs, mats)]
        invs = [inv - mm(p.astype(_BF16), inv).astype(_BF16) for inv, p in zip(invs, part)]
        s += 1
    return invs


def _mixer_kernel(x_ref, ph_ref, ch_ref, s0_ref, prm_ref, win_ref, wba_ref, wpool_ref, wout_ref,
                  y_ref, pool_out_ref, conv_out_ref, s_out_ref,
                  pool_buf, conv_buf, s_ref, zy_ref,
                  *, tile, chunk, pos0, apply_ln_in):
    t = pl.program_id(1)
    n_t = pl.num_programs(1)
    vec = lambda r, width: prm_ref[r:r + 1, 0:width]
    blk = min(tile, ROW_BLOCK)
    n_blk = tile // blk

    @pl.when(t == 0)
    def _():
        pool_buf[0:POOL_PAD, :] = ph_ref[0]
        conv_buf[0:CONV_PAD, :] = ch_ref[0]
        s_ref[...] = s0_ref[0]

    x = x_ref[0]
    if apply_ln_in:
        x = _layer_norm(x, vec(PRM_LN_IN_G, D_MODEL), vec(PRM_LN_IN_B, D_MODEL))
        y_ref[0] = x
    res_ref = y_ref if apply_ln_in else x_ref
    xb = x.astype(_BF16)

    raw = jnp.dot(xb, wba_ref[...], preferred_element_type=_F32)
    beta = _sigmoid(raw)
    g = -jnp.exp(vec(PRM_ALOG, V7X_LANES)) * _softplus(raw + vec(PRM_DT, V7X_LANES))
    rowc = lax.broadcasted_iota(jnp.int32, (tile, V7X_LANES), 0) & (chunk - 1)
    gc = g
    sh = 1
    while sh < chunk:
        gc = gc + jnp.where(rowc >= sh, pltpu.roll(gc, sh, 0), 0.0)
        sh *= 2
    gc_t = gc.T

    row = lax.broadcasted_iota(jnp.int32, (blk, V7X_LANES), 0)

    def pool_block(bi):
        r0 = bi * blk
        pos1 = row + (t * tile + (pos0 + 1 + r0))
        for gi, w in enumerate(POOL_WINDOWS):
            cnt = jnp.minimum(pos1, w).astype(_F32)
            for lo in range(gi * POOL_GROUP, (gi + 1) * POOL_GROUP, V7X_LANES):
                ext = pool_buf[r0:r0 + POOL_PAD + blk, lo:lo + V7X_LANES]
                acc = ext
                sh = 1
                while sh < w:
                    acc = acc + pltpu.roll(acc, sh, 0)
                    sh *= 2
                pool_buf[r0 + POOL_PAD:r0 + POOL_PAD + blk, lo:lo + V7X_LANES] = (
                    acc[POOL_PAD:] / cnt - ext[POOL_PAD:])

    def pool_matmuls():
        for gi in range(N_POOL_GROUPS):
            lo, hi = gi * POOL_GROUP, (gi + 1) * POOL_GROUP
            mixed = pool_buf[POOL_PAD:POOL_PAD + tile, lo:hi]
            pool_buf[POOL_PAD:POOL_PAD + tile, lo:hi] = (
                _dot(mixed, wpool_ref[gi]) * prm_ref[PRM_PSCALE:PRM_PSCALE + 1, lo:hi])

    def conv_block(part, bi):
        r0 = bi * blk
        for j in range(part * N_HEADS, (part + 1) * N_HEADS):
            lo, hi = j * HEAD_D, (j + 1) * HEAD_D
            ext = conv_buf[r0:r0 + CONV_PAD + blk, lo:hi]
            acc = ext[CONV_PAD:] * prm_ref[PRM_CONV + CONV_WIDTH - 1:PRM_CONV + CONV_WIDTH, lo:hi]
            for d in range(1, CONV_WIDTH):
                tap = CONV_WIDTH - 1 - d
                acc = acc + (pltpu.roll(ext, d, 0)[CONV_PAD:]
                             * prm_ref[PRM_CONV + tap:PRM_CONV + tap + 1, lo:hi])
            act = acc * _sigmoid(acc)
            if part < 2:
                scale = lax.rsqrt(jnp.sum(act * act, axis=-1, keepdims=True) + L2_EPS)
                if part == 0:
                    scale = scale * (HEAD_D ** -0.5)
                act = act * scale
            conv_buf[r0 + CONV_PAD:r0 + CONV_PAD + blk, lo:hi] = act

    pool_buf[POOL_PAD:POOL_PAD + tile, :] = jnp.dot(xb, win_ref[:, 0:D_MODEL],
                                                    preferred_element_type=_F32)
    for part in range(3):
        lo = OFF_QKV + part * D_MODEL
        conv_buf[CONV_PAD:CONV_PAD + tile, part * D_MODEL:(part + 1) * D_MODEL] = jnp.dot(
            xb, win_ref[:, lo:lo + D_MODEL], preferred_element_type=_F32)
    zy_ref[...] = jnp.dot(xb, win_ref[:, OFF_Z:OFF_GA], preferred_element_type=_F32)

    new_pool = pool_buf[tile + 1:tile + POOL_PAD, :]
    pool_out_ref[0] = new_pool
    for bi in reversed(range(n_blk)):
        pool_block(bi)
    pool_matmuls()
    pool_buf[1:POOL_PAD, :] = new_pool

    new_conv = conv_buf[CONV_PAD + tile - (CONV_WIDTH - 1):CONV_PAD + tile, :]
    conv_out_ref[0] = new_conv
    for bi in reversed(range(n_blk)):
        for part in range(3):
            conv_block(part, bi)
    conv_buf[CONV_PAD - (CONV_WIDTH - 1):CONV_PAD, :] = new_conv

    ri = lax.broadcasted_iota(jnp.int32, (chunk, chunk), 0)
    ci = lax.broadcasted_iota(jnp.int32, (chunk, chunk), 1)
    incl = ri >= ci
    strict = ri > ci

    heads = range(N_HEADS)
    q_lanes = [slice(h * HEAD_D, (h + 1) * HEAD_D) for h in heads]
    k_lanes = [slice(D_MODEL + h * HEAD_D, D_MODEL + (h + 1) * HEAD_D) for h in heads]
    v_lanes = [slice(2 * D_MODEL + h * HEAD_D, 2 * D_MODEL + (h + 1) * HEAD_D) for h in heads]
    n_chunks = tile // chunk
    units = [(c, h) for c in range(n_chunks) for h in heads]

    crow = lambda c: slice(CONV_PAD + c * chunk, CONV_PAD + (c + 1) * chunk)
    trow = lambda c: slice(c * chunk, (c + 1) * chunk)
    gc_cs = [gc[trow(c), :] for c in range(n_chunks)]
    gc_lasts = [g[chunk - 1:chunk, :] for g in gc_cs]
    e_alls = [jnp.exp(g) for g in gc_cs]
    tail_alls = [jnp.exp(gl - g) for gl, g in zip(gc_lasts, gc_cs)]
    carry_alls = [jnp.exp(gl) for gl in gc_lasts]
    col = lambda arr, h: arr[:, N_HEADS + h:N_HEADS + h + 1]
    kbs = {u: conv_buf[crow(u[0]), k_lanes[u[1]]] * beta[trow(u[0]), u[1]:u[1] + 1] for u in units}
    decays = {(c, h): jnp.where(
        incl, jnp.exp(col(gc_cs[c], h) - gc_t[N_HEADS + h:N_HEADS + h + 1, trow(c)]), 0.0)
        for c, h in units}
    qks = {(c, h): _dot_nt(jnp.concatenate([conv_buf[crow(c), q_lanes[h]], kbs[c, h]], axis=0),
                           conv_buf[crow(c), k_lanes[h]]) for c, h in units}
    attns = {u: (qks[u][0:chunk] * decays[u]).astype(_BF16) for u in units}
    t_invs = dict(zip(units, _unit_lower_inverses(
        [jnp.where(strict, qks[u][chunk:2 * chunk] * decays[u], 0.0).astype(_BF16) for u in units],
        chunk)))

    s_mats = [s_ref[h] for h in heads]
    for c in range(n_chunks):
        r0, r1 = c * chunk, (c + 1) * chunk
        e_cols = [col(e_alls[c], h) for h in heads]
        lhs = [jnp.concatenate([conv_buf[crow(c), q_lanes[h]] * e_cols[h], kbs[c, h] * e_cols[h]],
                               axis=0) for h in heads]
        k_tails = [conv_buf[crow(c), k_lanes[h]] * col(tail_alls[c], h) for h in heads]
        qss = [_dot(lhs[h], s_mats[h]) for h in heads]
        v_news = [_dot(t_invs[c, h], conv_buf[crow(c), v_lanes[h]] * beta[r0:r1, h:h + 1]
                       - qss[h][chunk:2 * chunk]) for h in heads]
        s_mats = [s_mats[h] * col(carry_alls[c], h) + _dot_tn(k_tails[h], v_news[h]) for h in heads]
        for h in heads:
            o = qss[h][0:chunk] + _dot(attns[c, h], v_news[h])
            o = (o * lax.rsqrt(jnp.mean(o * o, axis=-1, keepdims=True) + RMS_EPS)
                 * vec(PRM_OGAIN, HEAD_D))
            zc = zy_ref[r0:r1, q_lanes[h]]
            zy_ref[r0:r1, q_lanes[h]] = o * (zc * _sigmoid(zc))
    for h in heads:
        s_ref[h] = s_mats[h]

    ga = jnp.dot(xb, win_ref[:, OFF_GA:OFF_GB], preferred_element_type=_F32)
    merged = _sigmoid(ga) * pool_buf[POOL_PAD:POOL_PAD + tile, :]
    gb = jnp.dot(xb, win_ref[:, OFF_GB:OFF_BETA], preferred_element_type=_F32)
    merged = merged + _sigmoid(gb) * zy_ref[...]
    y = ALPHA * res_ref[0] + _dot(merged, wout_ref[...])
    y_ref[0] = _layer_norm(y, vec(PRM_LN1_G, D_MODEL), vec(PRM_LN1_B, D_MODEL))

    @pl.when(t == n_t - 1)
    def _():
        s_out_ref[0] = s_ref[...]


def _ffn_kernel(x_ref, prm_ref, w1_ref, w2_ref, y_ref):
    vec = lambda r: prm_ref[r:r + 1, 0:D_MODEL]
    x = x_ref[...]
    xb = x.astype(_BF16)
    acc = ALPHA * x + vec(FPRM_B2)
    for j in range(D_FF // D_MODEL):
        lo, hi = j * D_MODEL, (j + 1) * D_MODEL
        h = (jnp.dot(xb, w1_ref[:, lo:hi], preferred_element_type=_F32)
             + prm_ref[FPRM_B1:FPRM_B1 + 1, lo:hi])
        h = jnp.square(jnp.maximum(h, 0.0))
        acc = acc + jnp.dot(h.astype(_BF16), w2_ref[lo:hi, :], preferred_element_type=_F32)
    y_ref[...] = _layer_norm(acc, vec(FPRM_LN2_G), vec(FPRM_LN2_B))


def _layer_spec(shape, layer):
    return pl.BlockSpec((None,) + shape, lambda *_: (layer,) + (0,) * len(shape),
                        pipeline_mode=pl.Buffered(1))


def _mixer_call(x, pool_hist, conv_hist, s0, lw, *, layer, chunk, pos0):
    apply_ln_in = layer == 0
    nb, seq, _ = x.shape
    tile = min(seq, MIXER_TILE)
    n_t = seq // tile
    kern = functools.partial(_mixer_kernel, tile=tile, chunk=chunk, pos0=pos0,
                             apply_ln_in=apply_ln_in)
    per_stream = lambda shape: pl.BlockSpec((1,) + shape, lambda b, t: (b,) + (0,) * len(shape))
    in_specs = [
        pl.BlockSpec((1, tile, D_MODEL), lambda b, t: (b, t, 0)),
        per_stream((POOL_PAD, D_MODEL)),
        per_stream((CONV_PAD, CONV_CH)),
        per_stream((N_HEADS, HEAD_D, HEAD_D)),
        _layer_spec((PRM_ROWS, CONV_CH), layer),
        _layer_spec((D_MODEL, MAIN_WIDTH), layer),
        _layer_spec((D_MODEL, V7X_LANES), layer),
        _layer_spec((N_POOL_GROUPS, POOL_GROUP, POOL_GROUP), layer),
        _layer_spec((D_MODEL, D_MODEL), layer),
    ]
    out_shape = (
        jax.ShapeDtypeStruct((nb, seq, D_MODEL), _F32),
        jax.ShapeDtypeStruct((nb, POOL_STATE, D_MODEL), _F32),
        jax.ShapeDtypeStruct((nb, CONV_WIDTH - 1, CONV_CH), _F32),
        jax.ShapeDtypeStruct((nb, N_HEADS, HEAD_D, HEAD_D), _F32),
    )
    out_specs = (
        pl.BlockSpec((1, tile, D_MODEL), lambda b, t: (b, t, 0)),
        per_stream((POOL_STATE, D_MODEL)),
        per_stream((CONV_WIDTH - 1, CONV_CH)),
        per_stream((N_HEADS, HEAD_D, HEAD_D)),
    )
    scratch = [
        pltpu.VMEM((POOL_PAD + tile, D_MODEL), _F32),
        pltpu.VMEM((CONV_PAD + tile, CONV_CH), _F32),
        pltpu.VMEM((N_HEADS, HEAD_D, HEAD_D), _F32),
        pltpu.VMEM((tile, D_MODEL), _F32),
    ]
    return pl.pallas_call(
        kern,
        grid=(nb, n_t),
        in_specs=in_specs,
        out_specs=out_specs,
        out_shape=out_shape,
        scratch_shapes=scratch,
        compiler_params=pltpu.CompilerParams(
            dimension_semantics=("arbitrary", "arbitrary"),
            vmem_limit_bytes=V7X_VMEM_LIMIT_BYTES),
        name="mixer",
    )(x, pool_hist, conv_hist, s0, *lw)


def _ffn_call(x, prm, w1, w2, *, layer):
    n = x.shape[0]
    tile = min(n, FFN_TILE)
    row_spec = pl.BlockSpec((tile, D_MODEL), lambda i: (i, 0))
    return pl.pallas_call(
        _ffn_kernel,
        grid=(n // tile,),
        in_specs=[row_spec, _layer_spec((FPRM_ROWS, D_FF), layer),
                  _layer_spec((D_MODEL, D_FF), layer), _layer_spec((D_FF, D_MODEL), layer)],
        out_specs=row_spec,
        out_shape=jax.ShapeDtypeStruct((n, D_MODEL), _F32),
        compiler_params=pltpu.CompilerParams(
            dimension_semantics=("arbitrary",),
            vmem_limit_bytes=V7X_VMEM_LIMIT_BYTES),
        name="ffn",
    )(x, prm, w1, w2)


def kernel(x_prompt, x_sample, state_pool, state_conv, state_delta, ln_in_g, ln_in_b, w_in, conv_w, a_log,
           dt_bias, o_gain, w_pool, pool_scale, w_out, ln1_g, ln1_b, w_ff1, b_ff1, w_ff2, b_ff2, ln2_g, ln2_b):
    nbp, seq_p, _ = x_prompt.shape
    nbs, seq_s, _ = x_sample.shape
    rows = lambda v: v.reshape(DEPTH, 1, -1).astype(_F32)
    both = lambda v: jnp.broadcast_to(v.reshape(1, 1, -1).astype(_F32), (DEPTH, 1, v.size))
    gate_lanes = lambda v: jnp.pad(v.astype(_F32), ((0, 0), (N_HEADS, 0))).reshape(DEPTH, 1, -1)

    def slab(entries, n_rows, width):
        padded = [jnp.pad(e, ((0, 0), (0, 0), (0, width - e.shape[2]))) for e in entries]
        out = jnp.concatenate(padded, axis=1)
        return jnp.pad(out, ((0, 0), (0, n_rows - out.shape[1]), (0, 0)))

    mixer_prm = slab([conv_w.astype(_F32), both(ln_in_g), both(ln_in_b), gate_lanes(a_log),
                      gate_lanes(dt_bias), rows(o_gain), rows(pool_scale), rows(ln1_g), rows(ln1_b)],
                     PRM_ROWS, CONV_CH)
    w_ba = jnp.pad(w_in[:, :, OFF_BETA:], ((0, 0), (0, 0), (0, V7X_LANES - 2 * N_HEADS)))
    lw = (mixer_prm, w_in.astype(_BF16), w_ba.astype(_BF16), w_pool.astype(_BF16), w_out.astype(_BF16))
    fw = (slab([rows(b_ff1), rows(b_ff2), rows(ln2_g), rows(ln2_b)], FPRM_ROWS, D_FF),
          w_ff1.astype(_BF16), w_ff2.astype(_BF16))

    xp, xs = x_prompt, x_sample
    outs = {k: [] for k in ("pp", "cp", "sp", "ps", "cs", "ss")}
    zeros_pool = jnp.zeros((nbp, POOL_PAD, D_MODEL), _F32)
    zeros_conv = jnp.zeros((nbp, CONV_PAD, CONV_CH), _F32)
    zeros_s = jnp.zeros((nbp, N_HEADS, HEAD_D, HEAD_D), _F32)
    pool_hist = jnp.pad(state_pool, ((0, 0), (0, 0), (POOL_PAD - POOL_STATE, 0), (0, 0)))
    conv_hist = jnp.pad(state_conv, ((0, 0), (0, 0), (CONV_PAD - (CONV_WIDTH - 1), 0), (0, 0)))
    for l in range(DEPTH):
        xp, pp, cp, sp = _mixer_call(xp, zeros_pool, zeros_conv, zeros_s, lw,
                                     layer=l, chunk=PROMPT_CHUNK, pos0=0)
        xp = _ffn_call(xp.reshape(nbp * seq_p, D_MODEL), *fw, layer=l).reshape(nbp, seq_p, D_MODEL)

        xs, ps, cs, ss = _mixer_call(xs, pool_hist[l], conv_hist[l], state_delta[l], lw,
                                     layer=l, chunk=min(CHUNK, seq_s), pos0=PAST_LEN)
        xs = _ffn_call(xs.reshape(nbs * seq_s, D_MODEL), *fw, layer=l).reshape(nbs, seq_s, D_MODEL)

        for k, v in zip(("pp", "cp", "sp", "ps", "cs", "ss"), (pp, cp, sp, ps, cs, ss)):
            outs[k].append(v)
    return (xp, xs, jnp.stack(outs["pp"]), jnp.stack(outs["cp"]), jnp.stack(outs["sp"]),
            jnp.stack(outs["ps"]), jnp.stack(outs["cs"]), jnp.stack(outs["ss"]))
```

```python
import functools
import math

import jax
import jax.numpy as jnp
from jax import lax
from jax.experimental import pallas as pl
from jax.experimental.pallas import tpu as pltpu

D_MODEL = 1024
DEPTH = 2
PAST_LEN = 1024
CHUNK = 64
PROMPT_CHUNK = 128
POOL_WINDOWS = (2, 4, 8, 16)
N_POOL_GROUPS = 4
POOL_GROUP = D_MODEL // N_POOL_GROUPS
POOL_STATE = 15
HEAD_D = 128
N_HEADS = D_MODEL // HEAD_D
CONV_WIDTH = 4
CONV_CH = 3 * D_MODEL
D_FF = 4 * D_MODEL
ALPHA = (2 * DEPTH) ** 0.25
LN_EPS = 1e-5
RMS_EPS = 1e-6
L2_EPS = 1e-6
OFF_QKV = D_MODEL
OFF_Z = OFF_QKV + CONV_CH
OFF_GA = OFF_Z + D_MODEL
OFF_GB = OFF_GA + D_MODEL
OFF_BETA = OFF_GB + D_MODEL
MAIN_WIDTH = OFF_BETA
NEG_LOG2_E = -1.0 / math.log(2.0)

V7X_LANES = 128
V7X_SUBLANES = 8
BF16_ROWS_PER_TILE = 2 * V7X_SUBLANES
V7X_VMEM_LIMIT_BYTES = 60 * 1024 * 1024
WIDE_BLOCK = BF16_ROWS_PER_TILE

POOL_PAD = 16
CONV_PAD = 8
ROW_BLOCK = 128
MIXER_TILE = 256
PRM_CONV, PRM_LN_IN_G, PRM_LN_IN_B, PRM_ALOG, PRM_DT, PRM_OGAIN, PRM_PSCALE, PRM_LN1_G, PRM_LN1_B = (
    0, 4, 5, 6, 7, 8, 9, 10, 11)
PRM_ROWS = 16
FPRM_B1, FPRM_B2, FPRM_LN2_G, FPRM_LN2_B = 0, 1, 2, 3
FPRM_ROWS = 8
FFN_TILE = 1024

_BF16 = jnp.bfloat16
_F32 = jnp.float32


def _dot(a, b):
    return jnp.dot(a.astype(_BF16), b.astype(_BF16), preferred_element_type=_F32)


def _dot_nt(a, b):
    return lax.dot_general(a.astype(_BF16), b.astype(_BF16), (((1,), (1,)), ((), ())),
                           preferred_element_type=_F32)


def _dot_tn(a, b):
    return lax.dot_general(a.astype(_BF16), b.astype(_BF16), (((0,), (0,)), ((), ())),
                           preferred_element_type=_F32)


def _sigmoid(x):
    return 1.0 / (1.0 + jnp.exp2(x * NEG_LOG2_E))


def _softplus(x):
    return jnp.maximum(x, 0.0) + jnp.log1p(jnp.exp(-jnp.abs(x)))


def _layer_norm(x, g, b):
    mu = jnp.mean(x, axis=-1, keepdims=True)
    xc = x - mu
    var = jnp.mean(xc * xc, axis=-1, keepdims=True)
    return xc * lax.rsqrt(var + LN_EPS) * g + b


def _unit_lower_inverses(mats, n):
    as_mask = lambda cond: jnp.where(cond, 1.0, 0.0).astype(_BF16)
    mm = lambda a, b: jnp.dot(a, b, preferred_element_type=_F32)
    ri = lax.broadcasted_iota(jnp.int32, (n, n), 0)
    ci = lax.broadcasted_iota(jnp.int32, (n, n), 1)
    same = lambda s: (ri >> s) == (ci >> s)
    s = 1
    if n == V7X_LANES:
        wb = WIDE_BLOCK
        wi = lax.broadcasted_iota(jnp.int32, (wb, n), 0)
        wj = lax.broadcasted_iota(jnp.int32, (wb, n), 1) & (wb - 1)
        wsame = lambda lvl: (wi >> lvl) == (wj >> lvl)
        diag_blocks = as_mask(same(int(math.log2(wb))))

        def expand(w):
            return jnp.concatenate([w] * (n // wb), axis=0) * diag_blocks

        def gather(a):
            masked = a * diag_blocks
            out = masked[0:wb]
            for b in range(1, n // wb):
                out = out + masked[b * wb:(b + 1) * wb]
            return out

        wides = [gather(a) for a in mats]
        winvs = [as_mask(wi == wj) - w * as_mask(wsame(1)) for w in wides]
        while (1 << s) < wb:
            off = as_mask(wsame(s + 1) & jnp.logical_not(wsame(s)))
            part = [mm(winv, expand(w * off)) for winv, w in zip(winvs, wides)]
            winvs = [winv - mm(p.astype(_BF16), expand(winv)).astype(_BF16)
                     for winv, p in zip(winvs, part)]
            s += 1
        invs = [expand(winv) for winv in winvs]
    else:
        invs = [as_mask(ri == ci) - a * as_mask(same(1)) for a in mats]
    while (1 << s) < n:
        off = as_mask(same(s + 1) & jnp.logical_not(same(s)))
        part = [mm(inv, a * off) for inv, a in zip(invs, mats)]
        invs = [inv - mm(p.astype(_BF16), inv).astype(_BF16) for inv, p in zip(invs, part)]
        s += 1
    return invs


def _mixer_kernel(x_ref, ph_ref, ch_ref, s0_ref, prm_ref, win_ref, wba_ref, wpool_ref, wout_ref,
                  y_ref, pool_out_ref, conv_out_ref, s_out_ref,
                  pool_buf, conv_buf, s_ref, zy_ref,
                  *, tile, chunk, pos0, apply_ln_in):
    t = pl.program_id(1)
    n_t = pl.num_programs(1)
    vec = lambda r, width: prm_ref[r:r + 1, 0:width]
    blk = min(tile, ROW_BLOCK)
    n_blk = tile // blk

    @pl.when(t == 0)
    def _():
        pool_buf[0:POOL_PAD, :] = ph_ref[0]
        conv_buf[0:CONV_PAD, :] = ch_ref[0]
        s_ref[...] = s0_ref[0]

    x = x_ref[0]
    if apply_ln_in:
        x = _layer_norm(x, vec(PRM_LN_IN_G, D_MODEL), vec(PRM_LN_IN_B, D_MODEL))
        y_ref[0] = x
    res_ref = y_ref if apply_ln_in else x_ref
    xb = x.astype(_BF16)

    raw = jnp.dot(xb, wba_ref[...], preferred_element_type=_F32)
    beta = _sigmoid(raw)
    g = -jnp.exp(vec(PRM_ALOG, V7X_LANES)) * _softplus(raw + vec(PRM_DT, V7X_LANES))
    rowc = lax.broadcasted_iota(jnp.int32, (tile, V7X_LANES), 0) & (chunk - 1)
    gc = g
    sh = 1
    while sh < chunk:
        gc = gc + jnp.where(rowc >= sh, pltpu.roll(gc, sh, 0), 0.0)
        sh *= 2
    gc_t = gc.T

    row = lax.broadcasted_iota(jnp.int32, (blk, V7X_LANES), 0)

    def pool_block(bi):
        r0 = bi * blk
        pos1 = row + (t * tile + (pos0 + 1 + r0))
        for gi, w in enumerate(POOL_WINDOWS):
            cnt = jnp.minimum(pos1, w).astype(_F32)
            for lo in range(gi * POOL_GROUP, (gi + 1) * POOL_GROUP, V7X_LANES):
                ext = pool_buf[r0:r0 + POOL_PAD + blk, lo:lo + V7X_LANES]
                acc = ext
                sh = 1
                while sh < w:
                    acc = acc + pltpu.roll(acc, sh, 0)
                    sh *= 2
                pool_buf[r0 + POOL_PAD:r0 + POOL_PAD + blk, lo:lo + V7X_LANES] = (
                    acc[POOL_PAD:] / cnt - ext[POOL_PAD:])

    def pool_matmuls():
        for gi in range(N_POOL_GROUPS):
            lo, hi = gi * POOL_GROUP, (gi + 1) * POOL_GROUP
            mixed = pool_buf[POOL_PAD:POOL_PAD + tile, lo:hi]
            pool_buf[POOL_PAD:POOL_PAD + tile, lo:hi] = (
                _dot(mixed, wpool_ref[gi]) * prm_ref[PRM_PSCALE:PRM_PSCALE + 1, lo:hi])

    def conv_block(part, bi):
        r0 = bi * blk
        for j in range(part * N_HEADS, (part + 1) * N_HEADS):
            lo, hi = j * HEAD_D, (j + 1) * HEAD_D
            ext = conv_buf[r0:r0 + CONV_PAD + blk, lo:hi]
            acc = ext[CONV_PAD:] * prm_ref[PRM_CONV + CONV_WIDTH - 1:PRM_CONV + CONV_WIDTH, lo:hi]
            for d in range(1, CONV_WIDTH):
                tap = CONV_WIDTH - 1 - d
                acc = acc + (pltpu.roll(ext, d, 0)[CONV_PAD:]
                             * prm_ref[PRM_CONV + tap:PRM_CONV + tap + 1, lo:hi])
            act = acc * _sigmoid(acc)
            if part < 2:
                scale = lax.rsqrt(jnp.sum(act * act, axis=-1, keepdims=True) + L2_EPS)
                if part == 0:
                    scale = scale * (HEAD_D ** -0.5)
                act = act * scale
            conv_buf[r0 + CONV_PAD:r0 + CONV_PAD + blk, lo:hi] = act

    pool_buf[POOL_PAD:POOL_PAD + tile, :] = jnp.dot(xb, win_ref[:, 0:D_MODEL],
                                                    preferred_element_type=_F32)
    for part in range(3):
        lo = OFF_QKV + part * D_MODEL
        conv_buf[CONV_PAD:CONV_PAD + tile, part * D_MODEL:(part + 1) * D_MODEL] = jnp.dot(
            xb, win_ref[:, lo:lo + D_MODEL], preferred_element_type=_F32)
    zy_ref[...] = jnp.dot(xb, win_ref[:, OFF_Z:OFF_GA], preferred_element_type=_F32)

    new_pool = pool_buf[tile + 1:tile + POOL_PAD, :]
    pool_out_ref[0] = new_pool
    for bi in reversed(range(n_blk)):
        pool_block(bi)
    pool_matmuls()
    pool_buf[1:POOL_PAD, :] = new_pool

    new_conv = conv_buf[CONV_PAD + tile - (CONV_WIDTH - 1):CONV_PAD + tile, :]
    conv_out_ref[0] = new_conv
    for bi in reversed(range(n_blk)):
        for part in range(3):
            conv_block(part, bi)
    conv_buf[CONV_PAD - (CONV_WIDTH - 1):CONV_PAD, :] = new_conv

    ri = lax.broadcasted_iota(jnp.int32, (chunk, chunk), 0)
    ci = lax.broadcasted_iota(jnp.int32, (chunk, chunk), 1)
    incl = ri >= ci
    strict = ri > ci

    heads = range(N_HEADS)
    q_lanes = [slice(h * HEAD_D, (h + 1) * HEAD_D) for h in heads]
    k_lanes = [slice(D_MODEL + h * HEAD_D, D_MODEL + (h + 1) * HEAD_D) for h in heads]
    v_lanes = [slice(2 * D_MODEL + h * HEAD_D, 2 * D_MODEL + (h + 1) * HEAD_D) for h in heads]
    n_chunks = tile // chunk
    units = [(c, h) for c in range(n_chunks) for h in heads]

    crow = lambda c: slice(CONV_PAD + c * chunk, CONV_PAD + (c + 1) * chunk)
    trow = lambda c: slice(c * chunk, (c + 1) * chunk)
    gc_cs = [gc[trow(c), :] for c in range(n_chunks)]
    gc_lasts = [g[chunk - 1:chunk, :] for g in gc_cs]
    e_alls = [jnp.exp(g) for g in gc_cs]
    tail_alls = [jnp.exp(gl - g) for gl, g in zip(gc_lasts, gc_cs)]
    carry_alls = [jnp.exp(gl) for gl in gc_lasts]
    col = lambda arr, h: arr[:, N_HEADS + h:N_HEADS + h + 1]
    kbs = {u: conv_buf[crow(u[0]), k_lanes[u[1]]] * beta[trow(u[0]), u[1]:u[1] + 1] for u in units}
    decays = {(c, h): jnp.where(
        incl, jnp.exp(col(gc_cs[c], h) - gc_t[N_HEADS + h:N_HEADS + h + 1, trow(c)]), 0.0)
        for c, h in units}
    qks = {(c, h): _dot_nt(jnp.concatenate([conv_buf[crow(c), q_lanes[h]], kbs[c, h]], axis=0),
                           conv_buf[crow(c), k_lanes[h]]) for c, h in units}
    attns = {u: (qks[u][0:chunk] * decays[u]).astype(_BF16) for u in units}
    t_invs = dict(zip(units, _unit_lower_inverses(
        [jnp.where(strict, qks[u][chunk:2 * chunk] * decays[u], 0.0).astype(_BF16) for u in units],
        chunk)))

    s_mats = [s_ref[h] for h in heads]
    for c in range(n_chunks):
        r0, r1 = c * chunk, (c + 1) * chunk
        e_cols = [col(e_alls[c], h) for h in heads]
        lhs = [jnp.concatenate([conv_buf[crow(c), q_lanes[h]] * e_cols[h], kbs[c, h] * e_cols[h]],
                               axis=0) for h in heads]
        k_tails = [conv_buf[crow(c), k_lanes[h]] * col(tail_alls[c], h) for h in heads]
        qss = [_dot(lhs[h], s_mats[h]) for h in heads]
        v_news = [_dot(t_invs[c, h], conv_buf[crow(c), v_lanes[h]] * beta[r0:r1, h:h + 1]
                       - qss[h][chunk:2 * chunk]) for h in heads]
        s_mats = [s_mats[h] * col(carry_alls[c], h) + _dot_tn(k_tails[h], v_news[h]) for h in heads]
        for h in heads:
            o = qss[h][0:chunk] + _dot(attns[c, h], v_news[h])
            o = (o * lax.rsqrt(jnp.mean(o * o, axis=-1, keepdims=True) + RMS_EPS)
                 * vec(PRM_OGAIN, HEAD_D))
            zc = zy_ref[r0:r1, q_lanes[h]]
            zy_ref[r0:r1, q_lanes[h]] = o * (zc * _sigmoid(zc))
    for h in heads:
        s_ref[h] = s_mats[h]

    ga = jnp.dot(xb, win_ref[:, OFF_GA:OFF_GB], preferred_element_type=_F32)
    merged = _sigmoid(ga) * pool_buf[POOL_PAD:POOL_PAD + tile, :]
    gb = jnp.dot(xb, win_ref[:, OFF_GB:OFF_BETA], preferred_element_type=_F32)
    merged = merged + _sigmoid(gb) * zy_ref[...]
    y = ALPHA * res_ref[0] + _dot(merged, wout_ref[...])
    y_ref[0] = _layer_norm(y, vec(PRM_LN1_G, D_MODEL), vec(PRM_LN1_B, D_MODEL))

    @pl.when(t == n_t - 1)
    def _():
        s_out_ref[0] = s_ref[...]


def _ffn_kernel(x_ref, prm_ref, w1_ref, w2_ref, y_ref):
    vec = lambda r: prm_ref[r:r + 1, 0:D_MODEL]
    x = x_ref[...]
    xb = x.astype(_BF16)
    acc = ALPHA * x + vec(FPRM_B2)
    for j in range(D_FF // D_MODEL):
        lo, hi = j * D_MODEL, (j + 1) * D_MODEL
        h = (jnp.dot(xb, w1_ref[:, lo:hi], preferred_element_type=_F32)
             + prm_ref[FPRM_B1:FPRM_B1 + 1, lo:hi])
        h = jnp.square(jnp.maximum(h, 0.0))
        acc = acc + jnp.dot(h.astype(_BF16), w2_ref[lo:hi, :], preferred_element_type=_F32)
    y_ref[...] = _layer_norm(acc, vec(FPRM_LN2_G), vec(FPRM_LN2_B))


def _layer_spec(shape, layer):
    return pl.BlockSpec((None,) + shape, lambda *_: (layer,) + (0,) * len(shape),
                        pipeline_mode=pl.Buffered(1))


def _mixer_call(x, pool_hist, conv_hist, s0, lw, *, layer, chunk, pos0):
    apply_ln_in = layer == 0
    nb, seq, _ = x.shape
    tile = min(seq, MIXER_TILE)
    n_t = seq // tile
    kern = functools.partial(_mixer_kernel, tile=tile, chunk=chunk, pos0=pos0,
                             apply_ln_in=apply_ln_in)
    per_stream = lambda shape: pl.BlockSpec((1,) + shape, lambda b, t: (b,) + (0,) * len(shape))
    in_specs = [
        pl.BlockSpec((1, tile, D_MODEL), lambda b, t: (b, t, 0)),
        per_stream((POOL_PAD, D_MODEL)),
        per_stream((CONV_PAD, CONV_CH)),
        per_stream((N_HEADS, HEAD_D, HEAD_D)),
        _layer_spec((PRM_ROWS, CONV_CH), layer),
        _layer_spec((D_MODEL, MAIN_WIDTH), layer),
        _layer_spec((D_MODEL, V7X_LANES), layer),
        _layer_spec((N_POOL_GROUPS, POOL_GROUP, POOL_GROUP), layer),
        _layer_spec((D_MODEL, D_MODEL), layer),
    ]
    out_shape = (
        jax.ShapeDtypeStruct((nb, seq, D_MODEL), _F32),
        jax.ShapeDtypeStruct((nb, POOL_STATE, D_MODEL), _F32),
        jax.ShapeDtypeStruct((nb, CONV_WIDTH - 1, CONV_CH), _F32),
        jax.ShapeDtypeStruct((nb, N_HEADS, HEAD_D, HEAD_D), _F32),
    )
    out_specs = (
        pl.BlockSpec((1, tile, D_MODEL), lambda b, t: (b, t, 0)),
        per_stream((POOL_STATE, D_MODEL)),
        per_stream((CONV_WIDTH - 1, CONV_CH)),
        per_stream((N_HEADS, HEAD_D, HEAD_D)),
    )
    scratch = [
        pltpu.VMEM((POOL_PAD + tile, D_MODEL), _F32),
        pltpu.VMEM((CONV_PAD + tile, CONV_CH), _F32),
        pltpu.VMEM((N_HEADS, HEAD_D, HEAD_D), _F32),
        pltpu.VMEM((tile, D_MODEL), _F32),
    ]
    return pl.pallas_call(
        kern,
        grid=(nb, n_t),
        in_specs=in_specs,
        out_specs=out_specs,
        out_shape=out_shape,
        scratch_shapes=scratch,
        compiler_params=pltpu.CompilerParams(
            dimension_semantics=("arbitrary", "arbitrary"),
            vmem_limit_bytes=V7X_VMEM_LIMIT_BYTES),
        name="mixer",
    )(x, pool_hist, conv_hist, s0, *lw)


def _ffn_call(x, prm, w1, w2, *, layer):
    n = x.shape[0]
    tile = min(n, FFN_TILE)
    row_spec = pl.BlockSpec((tile, D_MODEL), lambda i: (i, 0))
    return pl.pallas_call(
        _ffn_kernel,
        grid=(n // tile,),
        in_specs=[row_spec, _layer_spec((FPRM_ROWS, D_FF), layer),
                  _layer_spec((D_MODEL, D_FF), layer), _layer_spec((D_FF, D_MODEL), layer)],
        out_specs=row_spec,
        out_shape=jax.ShapeDtypeStruct((n, D_MODEL), _F32),
        compiler_params=pltpu.CompilerParams(
            dimension_semantics=("arbitrary",),
            vmem_limit_bytes=V7X_VMEM_LIMIT_BYTES),
        name="ffn",
    )(x, prm, w1, w2)


def kernel(x_prompt, x_sample, state_pool, state_conv, state_delta, ln_in_g, ln_in_b, w_in, conv_w, a_log,
           dt_bias, o_gain, w_pool, pool_scale, w_out, ln1_g, ln1_b, w_ff1, b_ff1, w_ff2, b_ff2, ln2_g, ln2_b):
    nbp, seq_p, _ = x_prompt.shape
    nbs, seq_s, _ = x_sample.shape
    rows = lambda v: v.reshape(DEPTH, 1, -1).astype(_F32)
    both = lambda v: jnp.broadcast_to(v.reshape(1, 1, -1).astype(_F32), (DEPTH, 1, v.size))
    gate_lanes = lambda v: jnp.pad(v.astype(_F32), ((0, 0), (N_HEADS, 0))).reshape(DEPTH, 1, -1)

    def slab(entries, n_rows, width):
        padded = [jnp.pad(e, ((0, 0), (0, 0), (0, width - e.shape[2]))) for e in entries]
        out = jnp.concatenate(padded, axis=1)
        return jnp.pad(out, ((0, 0), (0, n_rows - out.shape[1]), (0, 0)))

    mixer_prm = slab([conv_w.astype(_F32), both(ln_in_g), both(ln_in_b), gate_lanes(a_log),
                      gate_lanes(dt_bias), rows(o_gain), rows(pool_scale), rows(ln1_g), rows(ln1_b)],
                     PRM_ROWS, CONV_CH)
    w_ba = jnp.pad(w_in[:, :, OFF_BETA:], ((0, 0), (0, 0), (0, V7X_LANES - 2 * N_HEADS)))
    lw = (mixer_prm, w_in.astype(_BF16), w_ba.astype(_BF16), w_pool.astype(_BF16), w_out.astype(_BF16))
    fw = (slab([rows(b_ff1), rows(b_ff2), rows(ln2_g), rows(ln2_b)], FPRM_ROWS, D_FF),
          w_ff1.astype(_BF16), w_ff2.astype(_BF16))

    xp, xs = x_prompt, x_sample
    outs = {k: [] for k in ("pp", "cp", "sp", "ps", "cs", "ss")}
    zeros_pool = jnp.zeros((nbp, POOL_PAD, D_MODEL), _F32)
    zeros_conv = jnp.zeros((nbp, CONV_PAD, CONV_CH), _F32)
    zeros_s = jnp.zeros((nbp, N_HEADS, HEAD_D, HEAD_D), _F32)
    pool_hist = jnp.pad(state_pool, ((0, 0), (0, 0), (POOL_PAD - POOL_STATE, 0), (0, 0)))
    conv_hist = jnp.pad(state_conv, ((0, 0), (0, 0), (CONV_PAD - (CONV_WIDTH - 1), 0), (0, 0)))
    for l in range(DEPTH):
        xp, pp, cp, sp = _mixer_call(xp, zeros_pool, zeros_conv, zeros_s, lw,
                                     layer=l, chunk=PROMPT_CHUNK, pos0=0)
        xp = _ffn_call(xp.reshape(nbp * seq_p, D_MODEL), *fw, layer=l).reshape(nbp, seq_p, D_MODEL)

        xs, ps, cs, ss = _mixer_call(xs, pool_hist[l], conv_hist[l], state_delta[l], lw,
                                     layer=l, chunk=min(CHUNK, seq_s), pos0=PAST_LEN)
        xs = _ffn_call(xs.reshape(nbs * seq_s, D_MODEL), *fw, layer=l).reshape(nbs, seq_s, D_MODEL)

        for k, v in zip(("pp", "cp", "sp", "ps", "cs", "ss"), (pp, cp, sp, ps, cs, ss)):
            outs[k].append(v)
    return (xp, xs, jnp.stack(outs["pp"]), jnp.stack(outs["cp"]), jnp.stack(outs["sp"]),
            jnp.stack(outs["ps"]), jnp.stack(outs["cs"]), jnp.stack(outs["ss"]))
```
